```python
import math
import jax, jax.numpy as jnp
from jax import lax
import numpy as np

D_MODEL = 1024
BATCH = 2
SEQ = 8192
DEPTH = 4
DEC_BATCH = 32
DEC_SEQ = 1
PAST_LEN = 8192
PAGE_SIZE = 128

D_MIX = D_MODEL
D_POOL = D_MIX // 2
POOL_WINDOWS = (2, 4, 8, 16)
POOL_GW = D_POOL // len(POOL_WINDOWS)
POOL_BUF = max(POOL_WINDOWS) - 1
D_ATTN = D_MIX - D_POOL
HEAD_DIM = 64
N_HEADS = D_ATTN // HEAD_DIM
N_KV = 2
Q_PER_KV = N_HEADS // N_KV
KV_W = N_KV * HEAD_DIM
ROPE_DIM = HEAD_DIM // 4
ROPE_THETA = 500000.0
CMP_LEN = 32
CMP_STRIDE = 16
SEL_LEN = 64
N_SELECT = 16
WINDOW = 512
Q_BLOCK = 128
D_PLE = 256
EPS = 1e-6
BIG = 1e30

SPLIT_SIZES = (D_POOL, D_POOL, D_ATTN, KV_W, KV_W, KV_W, KV_W, KV_W, KV_W, 3 * N_HEADS, D_ATTN)
D_IN = sum(SPLIT_SIZES)
SPLIT_POINTS = tuple(sum(SPLIT_SIZES[:i + 1]) for i in range(len(SPLIT_SIZES) - 1))

kernel_name = 'hymba_pool_nsa_decoder_step'


def rmsnorm(x, g):
    xf = x.astype(jnp.float32)
    y = xf * lax.rsqrt(jnp.mean(xf * xf, axis=-1, keepdims=True) + EPS)
    return (y * g.astype(jnp.float32)).astype(x.dtype)


def rope_partial(x, pos):
    half = ROPE_DIM // 2
    inv = ROPE_THETA ** (-jnp.arange(0, ROPE_DIM, 2, dtype=jnp.float32) / ROPE_DIM)
    ang = pos.astype(jnp.float32)[:, None] * inv[None, :]
    cos = jnp.cos(ang)[:, None, :]
    sin = jnp.sin(ang)[:, None, :]
    xr = x[..., :ROPE_DIM].astype(jnp.float32)
    x1, x2 = xr[..., :half], xr[..., half:]
    rot = jnp.concatenate([x1 * cos - x2 * sin, x2 * cos + x1 * sin], axis=-1)
    return jnp.concatenate([rot.astype(x.dtype), x[..., ROPE_DIM:]], axis=-1)


def pool_mix(u_ext, n_prev, pool_w, pool_scale):
    B, L, C = u_ext.shape
    uf = u_ext.astype(jnp.float32)
    c = jnp.concatenate([jnp.zeros((B, 1, C), jnp.float32), jnp.cumsum(uf, axis=1)], axis=1)
    r = jnp.arange(n_prev, L)
    c_hi = c[:, r + 1]
    outs = []
    for g, w in enumerate(POOL_WINDOWS):
        sl = slice(g * POOL_GW, (g + 1) * POOL_GW)
        start = jnp.maximum(r + 1 - w, 0)
        mean = (c_hi[..., sl] - c[:, start][..., sl]) / (r + 1 - start).astype(jnp.float32)[None, :, None]
        d = mean - uf[:, n_prev:, sl]
        outs.append(d @ pool_w[g].astype(jnp.float32))
    y = jnp.concatenate(outs, axis=-1) * pool_scale.astype(jnp.float32)
    return y.astype(u_ext.dtype)


def compress(k_full, pe, w1, w2):
    B, T = k_full.shape[:2]
    n_cmp = (T - CMP_LEN) // CMP_STRIDE + 1
    idx = (jnp.arange(n_cmp) * CMP_STRIDE)[:, None] + jnp.arange(CMP_LEN)[None, :]
    blk = k_full[:, idx] + pe[None, None, :, None, :]
    flat = jnp.swapaxes(blk, 2, 3).reshape(B, n_cmp, N_KV, CMP_LEN * HEAD_DIM)
    return jax.nn.silu(flat @ w1) @ w2


def sel_blocks(k_full):
    B, T = k_full.shape[:2]
    n_sel = -(-T // SEL_LEN)
    kp = jnp.pad(k_full, ((0, 0), (0, n_sel * SEL_LEN - T), (0, 0), (0, 0)))
    return kp.reshape(B, n_sel, SEL_LEN, N_KV, HEAD_DIM).transpose(0, 3, 1, 2, 4)


def masked_softmax(s, mask):
    s = jnp.where(mask, s, -BIG)
    return jnp.where(mask, jax.nn.softmax(s, axis=-1), 0.0)


def nsa_block(q, qpos, kc, vc, ks_blk, vs_blk, kw, vw, kwpos, g_br):
    B, Q = q.shape[:2]
    n_cmp, n_sel = kc.shape[1], ks_blk.shape[2]
    scale = HEAD_DIM ** -0.5
    qg = q.reshape(B, Q, N_KV, Q_PER_KV, HEAD_DIM)
    cend = jnp.arange(n_cmp) * CMP_STRIDE + CMP_LEN - 1
    cmask = (cend[None, :] <= qpos[:, None])[None, :, None, None, :]
    s = jnp.einsum('bqgrd,bngd->bqgrn', qg, kc).astype(jnp.float32) * scale
    p_cmp = masked_softmax(s, cmask)
    o_cmp = jnp.einsum('bqgrn,bngd->bqgrd', p_cmp, vc.astype(jnp.float32))
    ci = jnp.arange(n_cmp)[:, None] * CMP_STRIDE
    sj = jnp.arange(n_sel)[None, :] * SEL_LEN
    cmp_to_sel = ((ci < sj + SEL_LEN) & (ci + CMP_LEN > sj)).astype(jnp.float32)
    imp = p_cmp.sum(axis=3) @ cmp_to_sel
    blk = jnp.arange(n_sel)[None, :]
    cur = (qpos // SEL_LEN)[:, None]
    forced = (blk == 0) | (blk == cur) | (blk == cur - 1)
    imp = jnp.where(forced[None, :, None], BIG, jnp.where((blk <= cur)[None, :, None], imp, -BIG))
    n_top = min(N_SELECT, n_sel)
    top_val, top_idx = lax.top_k(imp, n_top)
    bi = jnp.arange(B)[:, None, None, None]
    gi = jnp.arange(N_KV)[None, None, :, None]
    ksel = ks_blk[bi, gi, top_idx].reshape(B, Q, N_KV, n_top * SEL_LEN, HEAD_DIM)
    vsel = vs_blk[bi, gi, top_idx].reshape(B, Q, N_KV, n_top * SEL_LEN, HEAD_DIM)
    kpos = top_idx[..., None] * SEL_LEN + jnp.arange(SEL_LEN)
    smask = (top_val[..., None] > -0.5 * BIG) & (kpos <= qpos[None, :, None, None, None])
    smask = smask.reshape(B, Q, N_KV, n_top * SEL_LEN)[:, :, :, None, :]
    s = jnp.einsum('bqgrd,bqgmd->bqgrm', qg, ksel).astype(jnp.float32) * scale
    o_sel = jnp.einsum('bqgrm,bqgmd->bqgrd', masked_softmax(s, smask), vsel.astype(jnp.float32))
    wmask = ((kwpos[None, :] <= qpos[:, None]) & (kwpos[None, :] >= qpos[:, None] - WINDOW)
             & (kwpos[None, :] >= 0))[None, :, None, None, :]
    s = jnp.einsum('bqgrd,blgd->bqgrl', qg, kw).astype(jnp.float32) * scale
    o_win = jnp.einsum('bqgrl,blgd->bqgrd', masked_softmax(s, wmask), vw.astype(jnp.float32))
    g = g_br.reshape(B, Q, N_KV, Q_PER_KV, 3)
    o = o_cmp * g[..., 0:1] + o_sel * g[..., 1:2] + o_win * g[..., 2:3]
    return o.reshape(B, Q, N_HEADS, HEAD_DIM).astype(q.dtype)


def mixer_inputs(h, pos, norm_g, w_in, q_norm, k_norm):
    B, S, _ = h.shape
    z = rmsnorm(h, norm_g) @ w_in
    u, g_pool, q, kc, vc, ks, vs, kw, vw, g_br, g_attn = jnp.split(z, SPLIT_POINTS, axis=-1)
    kvh = lambda t: t.reshape(B, S, N_KV, HEAD_DIM)
    qk = lambda t, g: rope_partial(rmsnorm(t, g), pos)
    q = qk(q.reshape(B, S, N_HEADS, HEAD_DIM), q_norm)
    kc = qk(kvh(kc), k_norm[0])
    ks = qk(kvh(ks), k_norm[1])
    kw = qk(kvh(kw), k_norm[2])
    g_br = jax.nn.sigmoid(g_br.astype(jnp.float32)).reshape(B, S, N_HEADS, 3)
    return u, g_pool, q, kc, kvh(vc), ks, kvh(vs), kw, kvh(vw), g_br, g_attn


def mixer_output(h, pool_y, g_pool, attn, g_attn, p, w_out, ple_gate, ple_proj):
    B, S, _ = h.shape
    a = attn.reshape(B, S, D_ATTN) * jax.nn.silu(g_attn)
    m = jnp.concatenate([pool_y * jax.nn.silu(g_pool), a], axis=-1)
    h = h + m @ w_out
    return h + jax.nn.sigmoid(h @ ple_gate) * (p @ ple_proj)


def prompt_layer(h, p, norm_g, w_in, q_norm, k_norm, cmp_pe, cmp_w1, cmp_w2,
                 pool_w, pool_scale, w_out, ple_gate, ple_proj):
    B, S, _ = h.shape
    pos = jnp.arange(S)
    u, g_pool, q, kc, vc, ks, vs, kw, vw, g_br, g_attn = mixer_inputs(h, pos, norm_g, w_in, q_norm, k_norm)
    pool_y = pool_mix(u, 0, pool_w, pool_scale)
    ck = compress(kc, cmp_pe[0], cmp_w1[0], cmp_w2[0])
    cv = compress(vc, cmp_pe[1], cmp_w1[1], cmp_w2[1])
    ks_blk, vs_blk = sel_blocks(ks), sel_blocks(vs)
    pad = ((0, 0), (WINDOW, 0), (0, 0), (0, 0))
    kw_pad, vw_pad = jnp.pad(kw, pad), jnp.pad(vw, pad)

    def q_block(j):
        t0 = j * Q_BLOCK
        qb = lax.dynamic_slice_in_dim(q, t0, Q_BLOCK, axis=1)
        gb = lax.dynamic_slice_in_dim(g_br, t0, Q_BLOCK, axis=1)
        kwb = lax.dynamic_slice_in_dim(kw_pad, t0, WINDOW + Q_BLOCK, axis=1)
        vwb = lax.dynamic_slice_in_dim(vw_pad, t0, WINDOW + Q_BLOCK, axis=1)
        qpos = t0 + jnp.arange(Q_BLOCK)
        kwpos = t0 - WINDOW + jnp.arange(WINDOW + Q_BLOCK)
        return nsa_block(qb, qpos, ck, cv, ks_blk, vs_blk, kwb, vwb, kwpos, gb)

    attn = lax.map(q_block, jnp.arange(S // Q_BLOCK))
    attn = jnp.moveaxis(attn, 0, 1).reshape(B, S, N_HEADS, HEAD_DIM)
    h = mixer_output(h, pool_y, g_pool, attn, g_attn, p, w_out, ple_gate, ple_proj)
    kv_rows = jnp.stack([kc, vc, ks, vs], axis=2)
    win_rows = jnp.stack([kw, vw], axis=2)[:, S - min(WINDOW, S):]
    pool_rows = u[:, S - POOL_BUF:]
    return h, kv_rows, win_rows, pool_rows


def sample_layer(h, p, kv_past, win_buf, pool_buf, norm_g, w_in, q_norm, k_norm, cmp_pe, cmp_w1, cmp_w2,
                 pool_w, pool_scale, w_out, ple_gate, ple_proj):
    B, S, _ = h.shape
    past = kv_past.shape[1]
    n_win = win_buf.shape[1]
    pos = past + jnp.arange(S)
    u, g_pool, q, kc, vc, ks, vs, kw, vw, g_br, g_attn = mixer_inputs(h, pos, norm_g, w_in, q_norm, k_norm)
    u_ext = jnp.concatenate([pool_buf.astype(u.dtype), u], axis=1)
    pool_y = pool_mix(u_ext, POOL_BUF, pool_w, pool_scale)
    cat = lambda i, new: jnp.concatenate([kv_past[:, :, i].astype(new.dtype), new], axis=1)
    ck = compress(cat(0, kc), cmp_pe[0], cmp_w1[0], cmp_w2[0])
    cv = compress(cat(1, vc), cmp_pe[1], cmp_w1[1], cmp_w2[1])
    ks_blk, vs_blk = sel_blocks(cat(2, ks)), sel_blocks(cat(3, vs))
    new_win = jnp.stack([kw, vw], axis=2)
    win_ext = jnp.concatenate([win_buf.astype(new_win.dtype), new_win], axis=1)
    kwpos = past - n_win + jnp.arange(n_win + S)
    attn = nsa_block(q, pos, ck, cv, ks_blk, vs_blk, win_ext[:, :, 0], win_ext[:, :, 1], kwpos, g_br)
    h = mixer_output(h, pool_y, g_pool, attn, g_attn, p, w_out, ple_gate, ple_proj)
    kv_rows = jnp.stack([kc, vc, ks, vs], axis=2)
    return h, kv_rows, win_ext[:, S:], u_ext[:, S:]


def setup_inputs(seed: int = 0) -> dict:
    key = jax.random.key(seed)
    k = jax.random.split(key, 24)
    f32 = jnp.float32
    nrm = lambda kk, shape, sc: jax.random.normal(kk, shape, f32) * sc
    n_pages = PAST_LEN // PAGE_SIZE
    n_pool = (DEC_BATCH * n_pages * 5) // 4
    n_win = min(WINDOW, PAST_LEN)
    page_table = jax.random.permutation(k[7], n_pool)[:DEC_BATCH * n_pages].reshape(DEC_BATCH, n_pages).astype(jnp.int32)
    return {
        'x_prompt': nrm(k[0], (BATCH, SEQ, D_MODEL), 1.0),
        'x_sample': nrm(k[1], (DEC_BATCH, DEC_SEQ, D_MODEL), 1.0),
        'cache_kv': nrm(k[2], (DEPTH, n_pool, PAGE_SIZE, 4, N_KV, HEAD_DIM), 1.0),
        'state_win': nrm(k[3], (DEPTH, DEC_BATCH, n_win, 2, N_KV, HEAD_DIM), 1.0),
        'state_pool': nrm(k[4], (DEPTH, DEC_BATCH, POOL_BUF, D_POOL), 1.0),
        'page_table': page_table,
        'p_prompt': nrm(k[5], (DEPTH, BATCH, SEQ, D_PLE), 1.0),
        'p_sample': nrm(k[6], (DEPTH, DEC_BATCH, DEC_SEQ, D_PLE), 1.0),
        'norm_g': 1.0 + nrm(k[8], (DEPTH, D_MODEL), 0.1),
        'w_in': nrm(k[9], (DEPTH, D_MODEL, D_IN), D_MODEL ** -0.5),
        'q_norm': 1.0 + nrm(k[10], (DEPTH, HEAD_DIM), 0.1),
        'k_norm': 1.0 + nrm(k[11], (DEPTH, 3, HEAD_DIM), 0.1),
        'cmp_pe': nrm(k[12], (DEPTH, 2, CMP_LEN, HEAD_DIM), 0.1),
        'cmp_w1': nrm(k[13], (DEPTH, 2, CMP_LEN * HEAD_DIM, HEAD_DIM), (CMP_LEN * HEAD_DIM) ** -0.5),
        'cmp_w2': nrm(k[14], (DEPTH, 2, HEAD_DIM, HEAD_DIM), HEAD_DIM ** -0.5),
        'pool_w': nrm(k[15], (DEPTH, len(POOL_WINDOWS), POOL_GW, POOL_GW), POOL_GW ** -0.5),
        'pool_scale': 1.0 + nrm(k[16], (DEPTH, D_POOL), 0.1),
        'w_out': nrm(k[17], (DEPTH, D_MIX, D_MODEL), D_MIX ** -0.5),
        'ple_gate': nrm(k[18], (DEPTH, D_MODEL, D_MODEL), D_MODEL ** -0.5),
        'ple_proj': nrm(k[19], (DEPTH, D_PLE, D_MODEL), D_PLE ** -0.5),
    }


def reference(x_prompt, x_sample, cache_kv, state_win, state_pool, page_table, p_prompt, p_sample,
              norm_g, w_in, q_norm, k_norm, cmp_pe, cmp_w1, cmp_w2, pool_w, pool_scale, w_out,
              ple_gate, ple_proj):
    dec_b, n_pages = page_table.shape
    hp, hs = x_prompt, x_sample
    kv_p, kv_s, win_p, win_s, pool_p, pool_s = [], [], [], [], [], []
    for l in range(DEPTH):
        w = (norm_g[l], w_in[l], q_norm[l], k_norm[l], cmp_pe[l], cmp_w1[l], cmp_w2[l],
             pool_w[l], pool_scale[l], w_out[l], ple_gate[l], ple_proj[l])
        hp, a, b, c = prompt_layer(hp, p_prompt[l], *w)
        kv_past = cache_kv[l][page_table].reshape(dec_b, n_pages * PAGE_SIZE, 4, N_KV, HEAD_DIM)
        hs, d, e, f = sample_layer(hs, p_sample[l], kv_past, state_win[l], state_pool[l], *w)
        kv_p.append(a); win_p.append(b); pool_p.append(c)
        kv_s.append(d); win_s.append(e); pool_s.append(f)
    return (hp, hs, jnp.stack(kv_p), jnp.stack(kv_s), jnp.stack(win_p), jnp.stack(win_s),
            jnp.stack(pool_p), jnp.stack(pool_s))
```

```python
import functools

import jax
import jax.numpy as jnp
from jax import lax
from jax.experimental import pallas as pl
from jax.experimental.pallas import tpu as pltpu

F32 = jnp.float32
BF16 = jnp.bfloat16

D_MODEL = 1024
D_POOL = 512
POOL_WINDOWS = (2, 4, 8, 16)
POOL_GW = 128
POOL_BUF = 15
D_ATTN = 512
HEAD_DIM = 64
N_HEADS = 8
N_KV = 2
Q_PER_KV = 4
KV_W = 128
ROPE_DIM = 16
ROPE_THETA = 500000.0
CMP_LEN = 32
CMP_STRIDE = 16
SEL_LEN = 64
N_SELECT = 16
WINDOW = 512
D_PLE = 256
EPS = 1e-6
BIG = 1e30
NEG = -3.0e38
LANES = 128
D_INR = 2944
PAGE = 128
VMEM_LIMIT = 56 * 1024 * 1024


def _dot(a, b):
    return jnp.dot(a, b, preferred_element_type=F32)


def _dot_nt(a, b):
    return lax.dot_general(a, b, (((1,), (1,)), ((), ())), preferred_element_type=F32)


def _dot_hilo(x, w):
    hi = x.astype(BF16)
    lo = (x - hi.astype(F32)).astype(BF16)
    return _dot(hi, w) + _dot(lo, w)


def _silu(x):
    return x * jax.nn.sigmoid(x)


def _mixer_in_kernel(x_ref, cos_ref, s1_ref, s2_ref, ng_ref, w_ref, gain_ref, j_ref,
                     u_ref, gp_ref, q_ref, kv_ref, win_ref, ga_ref, gbr_ref,
                     kaug_ref, vsb_ref, kwb_ref, vwb_ref, *, tm, nper):
    x = x_ref[...]
    ms = jnp.mean(x * x, axis=-1, keepdims=True)
    xn = (x * lax.rsqrt(ms + EPS) * ng_ref[...]).astype(BF16)
    cos, s1, s2 = cos_ref[...], s1_ref[...], s2_ref[...]
    jm = j_ref[...]
    lane = lax.broadcasted_iota(jnp.int32, (tm, LANES), 1)
    lo_half = lane < HEAD_DIM

    def normrope(zc, gain):
        msq = _dot_hilo(zc * zc, jm)
        y = zc * lax.rsqrt(msq + EPS) * gain
        return y * cos + pltpu.roll(y, 8, 1) * s1 + pltpu.roll(y, LANES - 8, 1) * s2

    u_ref[...] = _dot(xn, w_ref[:, 0:512])
    gp_ref[...] = _dot(xn, w_ref[:, 512:1024])
    zq = _dot(xn, w_ref[:, 1024:1536])
    for c in range(4):
        yc = normrope(zq[:, c * LANES:(c + 1) * LANES], gain_ref[:, c * LANES:(c + 1) * LANES])
        yr = pltpu.roll(yc, HEAD_DIM, 1)
        if c // 2 == 0:
            h0 = jnp.where(lo_half, yc, 0.0)
            h1 = jnp.where(lo_half, yr, 0.0)
        else:
            h0 = jnp.where(lo_half, 0.0, yr)
            h1 = jnp.where(lo_half, 0.0, yc)
        q_ref[:, (2 * c) * LANES:(2 * c + 1) * LANES] = h0.astype(BF16)
        q_ref[:, (2 * c + 1) * LANES:(2 * c + 2) * LANES] = h1.astype(BF16)
    zkv = _dot(xn, w_ref[:, 1536:2304])
    kc = normrope(zkv[:, 0:128], gain_ref[:, 512:640])
    vc = zkv[:, 128:256]
    ks = normrope(zkv[:, 256:384], gain_ref[:, 640:768])
    vs = zkv[:, 384:512]
    kw = normrope(zkv[:, 512:640], gain_ref[:, 768:896])
    vw = zkv[:, 640:768]
    kv_ref[:, 0:128] = kc
    kv_ref[:, 128:256] = vc
    kv_ref[:, 256:384] = ks
    kv_ref[:, 384:512] = vs
    win_ref[:, 0:128] = kw
    win_ref[:, 128:256] = vw
    ga_ref[...] = _dot(xn, w_ref[:, 2304:2816])
    gbr_ref[...] = jax.nn.sigmoid(_dot(xn, w_ref[:, 2816:2944]))
    pos = (pl.program_id(0) % nper) * tm + lax.broadcasted_iota(jnp.int32, (tm, LANES), 0)
    kaug_ref[:, 0:128] = ks.astype(BF16)
    kaug_ref[:, 128:256] = jnp.where(lane == pos // SEL_LEN, 1.0, 0.0).astype(BF16)
    vsb_ref[...] = vs.astype(BF16)
    kwb_ref[...] = kw.astype(BF16)
    vwb_ref[...] = vw.astype(BF16)


def _mixer_in(x, tabs, ng, w, gain, jm, tm):
    n = x.shape[0]
    nper = tabs[0].shape[0] // tm
    row = lambda width: pl.BlockSpec((tm, width), lambda i: (i, 0))
    full = lambda a: pl.BlockSpec(a.shape, lambda i: (0,) * a.ndim)
    tab = pl.BlockSpec((tm, LANES), lambda i: (i % nper, 0))
    widths = [(512, F32), (512, F32), (1024, BF16), (512, F32), (256, F32), (512, F32), (128, F32),
              (256, BF16), (128, BF16), (128, BF16), (128, BF16)]
    return pl.pallas_call(
        functools.partial(_mixer_in_kernel, tm=tm, nper=nper),
        out_shape=[jax.ShapeDtypeStruct((n, wd), dt) for wd, dt in widths],
        grid=(n // tm,),
        in_specs=[row(D_MODEL), tab, tab, tab, full(ng), full(w), full(gain), full(jm)],
        out_specs=[row(wd) for wd, _ in widths],
        compiler_params=pltpu.CompilerParams(dimension_semantics=("arbitrary",),
                                             vmem_limit_bytes=VMEM_LIMIT),
        name="mixer_in",
    )(x, *tabs, ng, w, gain, jm)


def _compress_kernel(pt_ref, src_ref, wk_ref, wv_ref, pe_ref, w2_ref, out_ref,
                     buf_ref, bias_ref, sem_ref, *, n_pages, nb):
    b = pl.program_id(0)
    nchunk = n_pages * PAGE // CMP_STRIDE
    slot = b % 2

    def page_copy(bb, sl, j, i):
        return pltpu.make_async_copy(
            src_ref.at[pt_ref[bb, j], :, pl.ds(i * KV_W, KV_W)],
            buf_ref.at[sl, i, pl.ds(j * PAGE, PAGE), :],
            sem_ref.at[sl])

    def start_all(bb, sl):
        def body(j, c):
            page_copy(bb, sl, j, 0).start()
            page_copy(bb, sl, j, 1).start()
            return c
        lax.fori_loop(0, n_pages, body, 0)

    def wait_all(bb, sl):
        def body(j, c):
            page_copy(bb, sl, j, 0).wait()
            page_copy(bb, sl, j, 1).wait()
            return c
        lax.fori_loop(0, n_pages, body, 0)

    @pl.when(b == 0)
    def _():
        start_all(0, 0)
        acc = jnp.zeros((8, 256), F32)
        for t in range(CMP_STRIDE):
            a0 = jnp.broadcast_to(pe_ref[t:t + 1, :], (8, 256)).astype(BF16)
            a1 = jnp.broadcast_to(pe_ref[CMP_STRIDE + t:CMP_STRIDE + t + 1, :], (8, 256)).astype(BF16)
            pk0 = _dot(a0[:, 0:128], wk_ref[t])
            pk1 = _dot(a1[:, 0:128], wk_ref[t])
            pv0 = _dot(a0[:, 128:256], wv_ref[t])
            pv1 = _dot(a1[:, 128:256], wv_ref[t])
            acc = acc + jnp.concatenate([pk0[:, 0:128] + pk1[:, 128:256],
                                         pv0[:, 0:128] + pv1[:, 128:256]], axis=1)
        bias_ref[...] = acc

    @pl.when(b + 1 < nb)
    def _():
        start_all(b + 1, 1 - slot)

    wait_all(b, slot)

    acc_k = jnp.zeros((nchunk, 256), F32)
    acc_v = jnp.zeros((nchunk, 256), F32)
    for t in range(CMP_STRIDE):
        xk = buf_ref[slot, 0, pl.ds(t, nchunk, stride=CMP_STRIDE), :]
        xv = buf_ref[slot, 1, pl.ds(t, nchunk, stride=CMP_STRIDE), :]
        acc_k = acc_k + _dot(xk.astype(BF16), wk_ref[t])
        acc_v = acc_v + _dot(xv.astype(BF16), wv_ref[t])
    row = lax.broadcasted_iota(jnp.int32, (nchunk, LANES), 0)
    valid = row < nchunk - 1
    for i, acc in enumerate((acc_k, acc_v)):
        pre = acc[:, 0:128] + pltpu.roll(acc[:, 128:256], nchunk - 1, 0) + bias_ref[0:1, i * 128:(i + 1) * 128]
        hid = _silu(pre).astype(BF16)
        res = _dot(hid, w2_ref[i])
        out_ref[0, :, i * 128:(i + 1) * 128] = jnp.where(valid, res, 0.0).astype(BF16)


def _compress(page_ids, src, wk, wv, pe2, w2):
    nb, n_pages = page_ids.shape
    nchunk = n_pages * PAGE // CMP_STRIDE
    full = lambda a: pl.BlockSpec(a.shape, lambda b, pt: (0,) * a.ndim)
    return pl.pallas_call(
        functools.partial(_compress_kernel, n_pages=n_pages, nb=nb),
        out_shape=jax.ShapeDtypeStruct((nb, nchunk, 256), BF16),
        grid_spec=pltpu.PrefetchScalarGridSpec(
            num_scalar_prefetch=1,
            grid=(nb,),
            in_specs=[pl.BlockSpec(memory_space=pl.ANY), full(wk), full(wv), full(pe2), full(w2)],
            out_specs=pl.BlockSpec((1, nchunk, 256), lambda b, pt: (b, 0, 0)),
            scratch_shapes=[pltpu.VMEM((2, 2, n_pages * PAGE, KV_W), F32),
                            pltpu.VMEM((8, 256), F32),
                            pltpu.SemaphoreType.DMA((2,))],
        ),
        compiler_params=pltpu.CompilerParams(dimension_semantics=("arbitrary",),
                                             vmem_limit_bytes=VMEM_LIMIT),
        name="compress",
    )(page_ids, src, wk, wv, pe2, w2)


def _select_topk(x, blk, on_pick):
    nl = float(x.shape[1])
    blkf = blk.astype(F32)
    for it in range(N_SELECT):
        m = jnp.max(x, axis=1, keepdims=True)
        idx = jnp.min(jnp.where(x == m, blkf, nl), axis=1, keepdims=True)
        chosen = blkf == idx
        on_pick(it, idx, chosen, m > -0.5 * BIG)
        x = jnp.where(chosen, NEG, x)


def _importance(imp, blk, cur):
    forced = (blk == 0) | (blk == cur) | (blk == cur - 1)
    return jnp.where(forced, BIG, jnp.where(blk <= cur, imp, -BIG))


def _nsa_prompt_kernel(q_ref, gbr_ref, cc_ref, kaug_ref, vs_ref, kw_ref, vw_ref, c2s_ref,
                       o_ref, lhs_ref, *, tq, tk, seq):
    t0 = pl.program_id(1) * tq
    rows = Q_PER_KV * tq
    nc = cc_ref.shape[1]
    qpos_r = t0 + lax.broadcasted_iota(jnp.int32, (rows, 1), 0) % tq
    qpos = t0 + lax.broadcasted_iota(jnp.int32, (tq, 1), 0)
    blk = lax.broadcasted_iota(jnp.int32, (tq, LANES), 1)
    lane = lax.broadcasted_iota(jnp.int32, (tq, LANES), 1)
    lo_half = lane < HEAD_DIM
    nwin = WINDOW + tq
    kstart = pl.multiple_of(jnp.maximum(t0 - WINDOW, 0), tq)
    kt_last = t0 // tk

    for g in range(N_KV):
        for r in range(Q_PER_KV):
            h = Q_PER_KV * g + r
            lhs_ref[r * tq:(r + 1) * tq, 0:LANES] = q_ref[0, :, h * LANES:(h + 1) * LANES]
        qg = lhs_ref[:, 0:LANES]

        s = _dot_nt(qg, cc_ref[0, :, 0:128])
        cend = lax.broadcasted_iota(jnp.int32, (1, nc), 1) * CMP_STRIDE + (CMP_LEN - 1)
        cmask = cend <= qpos_r
        s = jnp.where(cmask, s, -BIG)
        e = jnp.where(cmask, jnp.exp(s - jnp.max(s, axis=1, keepdims=True)), 0.0)
        l = jnp.sum(e, axis=1, keepdims=True)
        p = e * jnp.where(l > 0.0, 1.0 / l, 0.0)
        o_cmp = _dot(p.astype(BF16), cc_ref[0, :, 128:256])
        psum = p[0:tq] + p[tq:2 * tq] + p[2 * tq:3 * tq] + p[3 * tq:4 * tq]
        imp = _dot_hilo(psum, c2s_ref[...])

        sel = [jnp.zeros((tq, LANES), jnp.bool_)]

        def on_pick(it, idx, chosen, valid):
            sel[0] = sel[0] | (chosen & valid)

        _select_topk(_importance(imp, blk, qpos // SEL_LEN), blk, on_pick)
        selbias = jnp.where(sel[0], 0.0, -BIG).astype(BF16)
        for r in range(Q_PER_KV):
            lhs_ref[r * tq:(r + 1) * tq, LANES:2 * LANES] = selbias
        lhs = lhs_ref[...]

        def sel_tile(kt, carry, masked):
            m, l_, acc = carry
            ks = kaug_ref[0, pl.ds(pl.multiple_of(kt * tk, tk), tk), :]
            s_ = _dot_nt(lhs, ks)
            if masked:
                kpos = kt * tk + lax.broadcasted_iota(jnp.int32, (1, tk), 1)
                s_ = jnp.where(kpos <= qpos_r, s_, -BIG)
            m_new = jnp.maximum(m, jnp.max(s_, axis=1, keepdims=True))
            a = jnp.exp(m - m_new)
            p_ = jnp.exp(s_ - m_new)
            l_ = a * l_ + jnp.sum(p_, axis=1, keepdims=True)
            acc = a * acc + _dot(p_.astype(BF16), vs_ref[0, pl.ds(pl.multiple_of(kt * tk, tk), tk), :])
            return m_new, l_, acc

        init = (jnp.full((rows, 1), NEG, F32), jnp.zeros((rows, 1), F32), jnp.zeros((rows, LANES), F32))
        carry = lax.fori_loop(0, kt_last, lambda kt, c: sel_tile(kt, c, False), init)
        _, l_, acc = sel_tile(kt_last, carry, True)
        o_sel = acc / l_

        kwpos = kstart + lax.broadcasted_iota(jnp.int32, (1, nwin), 1)
        wmask = (kwpos <= qpos_r) & (kwpos >= qpos_r - WINDOW)
        s = jnp.where(wmask, _dot_nt(qg, kw_ref[0, pl.ds(kstart, nwin), :]), -BIG)
        e = jnp.exp(s - jnp.max(s, axis=1, keepdims=True))
        o_win = _dot(e.astype(BF16), vw_ref[0, pl.ds(kstart, nwin), :]) / jnp.sum(e, axis=1, keepdims=True)

        pieces = []
        for r in range(Q_PER_KV):
            h = Q_PER_KV * g + r
            sl = slice(r * tq, (r + 1) * tq)
            oh = (o_cmp[sl] * gbr_ref[0, :, 3 * h:3 * h + 1] + o_sel[sl] * gbr_ref[0, :, 3 * h + 1:3 * h + 2]
                  + o_win[sl] * gbr_ref[0, :, 3 * h + 2:3 * h + 3])
            pieces.append(oh if (h % 2) == g else pltpu.roll(oh, HEAD_DIM, 1))
        for c in range(2):
            o_ref[0, :, (2 * g + c) * LANES:(2 * g + c + 1) * LANES] = jnp.where(lo_half, pieces[2 * c], pieces[2 * c + 1])


def _nsa_prompt(qpad, gbr, cc, kaug, vsb, kwb, vwb, c2s, tq, tk):
    bsz, seq = qpad.shape[:2]
    nc = cc.shape[1]
    per_b = lambda a: pl.BlockSpec((1,) + a.shape[1:], lambda b, i: (b, 0, 0))
    return pl.pallas_call(
        functools.partial(_nsa_prompt_kernel, tq=tq, tk=tk, seq=seq),
        out_shape=jax.ShapeDtypeStruct((bsz, seq, D_ATTN), F32),
        grid=(bsz, seq // tq),
        in_specs=[pl.BlockSpec((1, tq, N_HEADS * LANES), lambda b, i: (b, i, 0)),
                  pl.BlockSpec((1, tq, LANES), lambda b, i: (b, i, 0)),
                  per_b(cc), per_b(kaug), per_b(vsb), per_b(kwb), per_b(vwb),
                  pl.BlockSpec(c2s.shape, lambda b, i: (0, 0))],
        out_specs=pl.BlockSpec((1, tq, D_ATTN), lambda b, i: (b, i, 0)),
        scratch_shapes=[pltpu.VMEM((Q_PER_KV * tq, 2 * LANES), BF16)],
        compiler_params=pltpu.CompilerParams(dimension_semantics=("arbitrary", "arbitrary"),
                                             vmem_limit_bytes=VMEM_LIMIT),
        name="nsa_prompt",
    )(qpad, gbr, cc, kaug, vsb, kwb, vwb, c2s)


def _nsa_sample_cmp_kernel(q_ref, cc_ref, c2s_ref, ocmp_ref, idx_ref, *, past):
    nc = cc_ref.shape[1]
    nl = c2s_ref.shape[1]
    q = q_ref[0]
    s = _dot_nt(q, cc_ref[0, :, 0:128])
    cend = lax.broadcasted_iota(jnp.int32, (1, nc), 1) * CMP_STRIDE + (CMP_LEN - 1)
    cmask = cend <= past
    s = jnp.where(cmask, s, -BIG)
    e = jnp.where(cmask, jnp.exp(s - jnp.max(s, axis=1, keepdims=True)), 0.0)
    l = jnp.sum(e, axis=1, keepdims=True)
    p = e * jnp.where(l > 0.0, 1.0 / l, 0.0)
    ocmp_ref[0] = _dot(p.astype(BF16), cc_ref[0, :, 128:256])
    row = lax.broadcasted_iota(jnp.int32, (8, 1), 0)
    ps0 = jnp.sum(p[0:Q_PER_KV], axis=0, keepdims=True)
    ps1 = jnp.sum(p[Q_PER_KV:2 * Q_PER_KV], axis=0, keepdims=True)
    psum = jnp.where(row == 0, ps0, jnp.where(row == 1, ps1, 0.0))
    imp = _dot_hilo(psum, c2s_ref[...])
    blk = lax.broadcasted_iota(jnp.int32, (8, nl), 1)
    out_lane = lax.broadcasted_iota(jnp.int32, (8, LANES), 1)
    picks = [jnp.full((8, LANES), -1, jnp.int32)]

    def on_pick(it, idx, chosen, valid):
        picks[0] = jnp.where(out_lane == it, jnp.where(valid, idx.astype(jnp.int32), -1), picks[0])

    _select_topk(_importance(imp, blk, past // SEL_LEN), blk, on_pick)
    idx_ref[0] = picks[0]


def _nsa_sample_cmp(q16, cc, c2s, past):
    bsz = q16.shape[0]
    nc = cc.shape[1]
    return pl.pallas_call(
        functools.partial(_nsa_sample_cmp_kernel, past=past),
        out_shape=[jax.ShapeDtypeStruct((bsz, 16, LANES), F32), jax.ShapeDtypeStruct((bsz, 8, LANES), jnp.int32)],
        grid=(bsz,),
        in_specs=[pl.BlockSpec((1, 16, LANES), lambda b: (b, 0, 0)),
                  pl.BlockSpec((1, nc, 256), lambda b: (b, 0, 0)),
                  pl.BlockSpec(c2s.shape, lambda b: (0, 0))],
        out_specs=[pl.BlockSpec((1, 16, LANES), lambda b: (b, 0, 0)),
                   pl.BlockSpec((1, 8, LANES), lambda b: (b, 0, 0))],
        compiler_params=pltpu.CompilerParams(dimension_semantics=("arbitrary",)),
        name="nsa_sample_cmp",
    )(q16, cc, c2s)


def _nsa_sample_attn_kernel(pt_ref, idx_ref, cache_ref, q_ref, ocmp_ref, gate_ref, win_ref, new_ref,
                            o_ref, buf_ref, sem_ref, *, past, nb):
    b = pl.program_id(0)
    nblk = N_KV * N_SELECT
    npast_blk = past // SEL_LEN
    per_page = PAGE // SEL_LEN
    slot = b % 2

    def blk_copy(bb, sl, i):
        j = jnp.clip(idx_ref[bb, i], 0, npast_blk - 1)
        return pltpu.make_async_copy(
            cache_ref.at[pt_ref[bb, j // per_page], pl.ds((j % per_page) * SEL_LEN, SEL_LEN), pl.ds(256, 256)],
            buf_ref.at[sl, pl.ds(i * SEL_LEN, SEL_LEN), :],
            sem_ref.at[sl])

    def start_all(bb, sl):
        def body(i, c):
            blk_copy(bb, sl, i).start()
            return c
        lax.fori_loop(0, nblk, body, 0)

    def wait_all(bb, sl):
        def body(i, c):
            blk_copy(bb, sl, i).wait()
            return c
        lax.fori_loop(0, nblk, body, 0)

    @pl.when(b == 0)
    def _():
        start_all(0, 0)

    @pl.when(b + 1 < nb)
    def _():
        start_all(b + 1, 1 - slot)

    q = q_ref[0]
    qf = q.astype(F32)
    knew = new_ref[0, 0:1, :]
    ks_new = knew[:, 0:128].astype(BF16).astype(F32)
    vs_new = knew[:, 128:256].astype(BF16).astype(F32)
    kw_new = knew[:, 256:384].astype(BF16).astype(F32)
    vw_new = knew[:, 384:512].astype(BF16).astype(F32)
    row = lax.broadcasted_iota(jnp.int32, (16, 1), 0)

    s = _dot_nt(q, win_ref[0, :, 0:128].astype(BF16))
    s_new = jnp.sum(qf * kw_new, axis=1, keepdims=True)
    m = jnp.maximum(jnp.max(s, axis=1, keepdims=True), s_new)
    e = jnp.exp(s - m)
    e_new = jnp.exp(s_new - m)
    o_win = ((_dot(e.astype(BF16), win_ref[0, :, 128:256].astype(BF16)) + e_new.astype(BF16).astype(F32) * vw_new)
             / (jnp.sum(e, axis=1, keepdims=True) + e_new))

    wait_all(b, slot)

    nkeys = N_SELECT * SEL_LEN
    lane = lax.broadcasted_iota(jnp.int32, (1, nkeys), 1)
    s_new = jnp.sum(qf * ks_new, axis=1, keepdims=True)
    o_sel = jnp.zeros((16, LANES), F32)
    for g in range(N_KV):
        kpos = jnp.full((1, nkeys), past + 1, jnp.int32)
        for n in range(N_SELECT):
            j = idx_ref[b, g * N_SELECT + n]
            start = jnp.where((j >= 0) & (j < npast_blk), j * SEL_LEN, past + 1)
            kpos = jnp.where(lane // SEL_LEN == n, start + lane % SEL_LEN, kpos)
        kvb = buf_ref[slot, pl.ds(g * nkeys, nkeys), :]
        s = jnp.where(kpos <= past, _dot_nt(q, kvb[:, 0:128].astype(BF16)), -BIG)
        m = jnp.maximum(jnp.max(s, axis=1, keepdims=True), s_new)
        e = jnp.exp(s - m)
        e_new = jnp.exp(s_new - m)
        og = ((_dot(e.astype(BF16), kvb[:, 128:256].astype(BF16)) + e_new.astype(BF16).astype(F32) * vs_new)
              / (jnp.sum(e, axis=1, keepdims=True) + e_new))
        o_sel = jnp.where((row >= g * Q_PER_KV) & (row < (g + 1) * Q_PER_KV), og, o_sel)

    gt = gate_ref[0]
    o_ref[0] = ocmp_ref[0] * gt[:, 0:1] + o_sel * gt[:, 1:2] + o_win * gt[:, 2:3]


def _nsa_sample_attn(page_ids, idx, cache, q16, ocmp, gate16, state_win, new_rows, layer, past):
    bsz = q16.shape[0]
    nwin = state_win.shape[2]
    nblk = N_KV * N_SELECT
    return pl.pallas_call(
        functools.partial(_nsa_sample_attn_kernel, past=past, nb=bsz),
        out_shape=jax.ShapeDtypeStruct((bsz, 16, LANES), F32),
        grid_spec=pltpu.PrefetchScalarGridSpec(
            num_scalar_prefetch=2,
            grid=(bsz,),
            in_specs=[pl.BlockSpec(memory_space=pl.ANY),
                      pl.BlockSpec((1, 16, LANES), lambda b, pt, ix: (b, 0, 0)),
                      pl.BlockSpec((1, 16, LANES), lambda b, pt, ix: (b, 0, 0)),
                      pl.BlockSpec((1, 16, LANES), lambda b, pt, ix: (b, 0, 0)),
                      pl.BlockSpec((None, 1, nwin, 256), lambda b, pt, ix: (layer, b, 0, 0)),
                      pl.BlockSpec((1, 8, 512), lambda b, pt, ix: (b, 0, 0))],
            out_specs=pl.BlockSpec((1, 16, LANES), lambda b, pt, ix: (b, 0, 0)),
            scratch_shapes=[pltpu.VMEM((2, nblk * SEL_LEN, 256), F32),
                            pltpu.SemaphoreType.DMA((2,))],
        ),
        compiler_params=pltpu.CompilerParams(dimension_semantics=("arbitrary",)),
        name="nsa_sample_attn",
    )(page_ids, idx, cache, q16, ocmp, gate16, state_win, new_rows)


def _pool_kernel(u_ref, halo_ref, pw_ref, ps_ref, y_ref, ext_ref, *, tm):
    i = pl.program_id(1)
    hb = POOL_BUF + 1
    ext_ref[0:hb, :] = jnp.where(i > 0, halo_ref[0], 0.0)
    ext_ref[hb:hb + tm, :] = u_ref[0]
    r = i * tm + lax.broadcasted_iota(jnp.int32, (tm, 1), 0)
    for g, w in enumerate(POOL_WINDOWS):
        cols = slice(g * POOL_GW, (g + 1) * POOL_GW)
        acc = ext_ref[hb:hb + tm, cols]
        for k in range(1, w):
            acc = acc + ext_ref[hb - k:hb - k + tm, cols]
        cnt = jnp.minimum(r + 1, w).astype(F32)
        d = acc / cnt - ext_ref[hb:hb + tm, cols]
        y_ref[0, :, cols] = _dot(d.astype(BF16), pw_ref[g]) * ps_ref[:, cols]


def _pool(u, pw, ps, tm):
    bsz, seq = u.shape[:2]
    hb = POOL_BUF + 1
    ratio = tm // hb
    return pl.pallas_call(
        functools.partial(_pool_kernel, tm=tm),
        out_shape=jax.ShapeDtypeStruct(u.shape, F32),
        grid=(bsz, seq // tm),
        in_specs=[pl.BlockSpec((1, tm, D_POOL), lambda b, i: (b, i, 0)),
                  pl.BlockSpec((1, hb, D_POOL), lambda b, i: (b, jnp.maximum(i * ratio - 1, 0), 0)),
                  pl.BlockSpec(pw.shape, lambda b, i: (0, 0, 0)),
                  pl.BlockSpec(ps.shape, lambda b, i: (0, 0))],
        out_specs=pl.BlockSpec((1, tm, D_POOL), lambda b, i: (b, i, 0)),
        scratch_shapes=[pltpu.VMEM((hb + tm, D_POOL), F32)],
        compiler_params=pltpu.CompilerParams(dimension_semantics=("arbitrary", "arbitrary")),
        name="pool",
    )(u, u, pw, ps)


def _mixer_out_kernel(h_ref, py_ref, gp_ref, at_ref, ga_ref, p_ref, wo_ref, wg_ref, wp_ref, o_ref):
    mp = (py_ref[...] * _silu(gp_ref[...])).astype(BF16)
    ma = (at_ref[...] * _silu(ga_ref[...])).astype(BF16)
    h1 = h_ref[...] + _dot(mp, wo_ref[0:D_POOL, :]) + _dot(ma, wo_ref[D_POOL:D_POOL + D_ATTN, :])
    gate = jax.nn.sigmoid(_dot(h1.astype(BF16), wg_ref[...]))
    o_ref[...] = h1 + gate * _dot(p_ref[...].astype(BF16), wp_ref[...])


def _mixer_out(h, py, gp, at, ga, p, wo, wg, wp, tm):
    n = h.shape[0]
    row = lambda a: pl.BlockSpec((tm, a.shape[1]), lambda i: (i, 0))
    full = lambda a: pl.BlockSpec(a.shape, lambda i: (0, 0))
    return pl.pallas_call(
        _mixer_out_kernel,
        out_shape=jax.ShapeDtypeStruct(h.shape, F32),
        grid=(n // tm,),
        in_specs=[row(h), row(py), row(gp), row(at), row(ga), row(p), full(wo), full(wg), full(wp)],
        out_specs=pl.BlockSpec((tm, D_MODEL), lambda i: (i, 0)),
        compiler_params=pltpu.CompilerParams(dimension_semantics=("arbitrary",),
                                             vmem_limit_bytes=VMEM_LIMIT),
        name="mixer_out",
    )(h, py, gp, at, ga, p, wo, wg, wp)


def _rope_tables(pos):
    half = ROPE_DIM // 2
    inv = ROPE_THETA ** (-jnp.arange(0, ROPE_DIM, 2, dtype=F32) / ROPE_DIM)
    ang = pos.astype(F32)[:, None] * inv[None, :]
    cos, sin = jnp.cos(ang), jnp.sin(ang)
    n = pos.shape[0]
    rest = HEAD_DIM - ROPE_DIM
    c = jnp.concatenate([cos, cos, jnp.ones((n, rest), F32)], axis=1)
    s1 = jnp.concatenate([jnp.zeros((n, half), F32), sin, jnp.zeros((n, rest), F32)], axis=1)
    s2 = jnp.concatenate([-sin, jnp.zeros((n, half + rest), F32)], axis=1)
    return tuple(jnp.tile(t, (1, LANES // HEAD_DIM)) for t in (c, s1, s2))


def _cmp_to_sel(nc, n_cmp, nl):
    ci = jnp.arange(nc)[:, None] * CMP_STRIDE
    sj = jnp.arange(nl)[None, :] * SEL_LEN
    m = (ci < sj + SEL_LEN) & (ci + CMP_LEN > sj) & (jnp.arange(nc)[:, None] < n_cmp)
    return m.astype(BF16)


def _layer_params(l, norm_g, w_in, q_norm, k_norm, cmp_pe, cmp_w1, cmp_w2, pool_w, pool_scale, w_out,
                  ple_gate, ple_proj):
    w = w_in[l]
    pad = jnp.zeros((D_MODEL, D_INR - 2840), w.dtype)
    w_r = jnp.concatenate([w[:, 0:2304], w[:, 2328:2840], w[:, 2304:2328], pad], axis=1).astype(BF16)
    ones = jnp.ones((KV_W,), F32)
    gain = jnp.concatenate([jnp.tile(q_norm[l], N_HEADS) * (HEAD_DIM ** -0.5),
                            jnp.tile(k_norm[l, 0], N_KV), jnp.tile(k_norm[l, 1], N_KV),
                            jnp.tile(k_norm[l, 2], N_KV), ones, ones, ones])[None, :].astype(F32)
    eye = jnp.eye(N_KV, dtype=F32)

    def taps(w1):
        w1r = w1.reshape(2, CMP_STRIDE, HEAD_DIM, HEAD_DIM)
        return jnp.einsum('gh,atde->tgdahe', eye, w1r).reshape(CMP_STRIDE, KV_W, 2 * KV_W).astype(BF16)

    w2 = jnp.stack([jnp.kron(eye, cmp_w2[l, i]) for i in range(2)]).astype(BF16)
    pe2 = jnp.concatenate([jnp.tile(cmp_pe[l, 0], (1, N_KV)), jnp.tile(cmp_pe[l, 1], (1, N_KV))], axis=1)
    return dict(ng=norm_g[l][None, :], w=w_r, gain=gain, wk=taps(cmp_w1[l, 0]), wv=taps(cmp_w1[l, 1]),
                w2=w2, pe2=pe2, pw=pool_w[l].astype(BF16), ps=pool_scale[l][None, :],
                wo=w_out[l].astype(BF16), wg=ple_gate[l].astype(BF16), wp=ple_proj[l].astype(BF16))


def kernel(x_prompt, x_sample, cache_kv, state_win, state_pool, page_table, p_prompt, p_sample,
           norm_g, w_in, q_norm, k_norm, cmp_pe, cmp_w1, cmp_w2, pool_w, pool_scale, w_out,
           ple_gate, ple_proj):
    depth = w_in.shape[0]
    bsz, seq, _ = x_prompt.shape
    dec_b, n_pages = page_table.shape
    n_pool, page_size = cache_kv.shape[1:3]
    past = n_pages * page_size
    assert page_size == PAGE and seq % 512 == 0 and seq >= WINDOW + 128 and past >= WINDOW
    assert x_sample.shape[1] == 1 and past % SEL_LEN == 0
    n_sel_p = seq // SEL_LEN
    n_sel_s = past // SEL_LEN + 1
    assert N_SELECT <= n_sel_p <= LANES
    nl_s = -(-n_sel_s // LANES) * LANES
    nc_p, nc_s = seq // CMP_STRIDE, past // CMP_STRIDE

    tm = 256
    tabs_p = _rope_tables(jnp.arange(seq))
    tabs_s = _rope_tables(jnp.full((dec_b,), past, jnp.int32))
    jm = jnp.kron(jnp.eye(LANES // HEAD_DIM, dtype=F32), jnp.full((HEAD_DIM, HEAD_DIM), 1.0 / HEAD_DIM, F32)).astype(BF16)
    c2s_p = _cmp_to_sel(nc_p, nc_p - 1, LANES)
    c2s_s = _cmp_to_sel(nc_s, nc_s - 1, nl_s)
    ids_p = jnp.arange(bsz * seq // PAGE, dtype=jnp.int32).reshape(bsz, seq // PAGE)
    cache = cache_kv.reshape(depth * n_pool, PAGE, 512)
    win_state = state_win.reshape(depth, dec_b, state_win.shape[2], 256)

    hp = x_prompt.reshape(bsz * seq, D_MODEL)
    hs = x_sample.reshape(dec_b, D_MODEL)
    outs = [[] for _ in range(6)]
    for l in range(depth):
        prm = _layer_params(l, norm_g, w_in, q_norm, k_norm, cmp_pe, cmp_w1, cmp_w2, pool_w, pool_scale,
                            w_out, ple_gate, ple_proj)
        u, gp, qpad, kv, win, ga, gbr, kaug, vsb, kwb, vwb = _mixer_in(
            hp, tabs_p, prm['ng'], prm['w'], prm['gain'], jm, tm)
        cc = _compress(ids_p, kv.reshape(bsz * seq // PAGE, PAGE, 512), prm['wk'], prm['wv'], prm['pe2'], prm['w2'])
        r3 = lambda a: a.reshape(bsz, seq, a.shape[-1])
        attn = _nsa_prompt(r3(qpad), r3(gbr), cc, r3(kaug), r3(vsb), r3(kwb), r3(vwb), c2s_p, 128, 512)
        py = _pool(r3(u), prm['pw'], prm['ps'], tm)
        hp = _mixer_out(hp, py.reshape(bsz * seq, D_POOL), gp, attn.reshape(bsz * seq, D_ATTN), ga,
                        p_prompt[l].reshape(bsz * seq, D_PLE), prm['wo'], prm['wg'], prm['wp'], tm)
        outs[0].append(kv.reshape(bsz, seq, 4, N_KV, HEAD_DIM))
        outs[2].append(win.reshape(bsz, seq, 2, N_KV, HEAD_DIM)[:, seq - min(WINDOW, seq):])
        outs[4].append(r3(u)[:, seq - POOL_BUF:])

        u, gp, qpad, kv, win, ga, gbr, _, _, _, _ = _mixer_in(
            hs, tabs_s, prm['ng'], prm['w'], prm['gain'], jm, dec_b)
        ids_s = page_table + l * n_pool
        cc = _compress(ids_s, cache, prm['wk'], prm['wv'], prm['pe2'], prm['w2'])
        q16 = jnp.pad(qpad.reshape(dec_b, N_HEADS, LANES), ((0, 0), (0, 16 - N_HEADS), (0, 0)))
        ocmp, idx = _nsa_sample_cmp(q16, cc, c2s_s, past)
        gate16 = jnp.pad(gbr[:, :3 * N_HEADS].reshape(dec_b, N_HEADS, 3), ((0, 0), (0, 16 - N_HEADS), (0, LANES - 3)))
        new_rows = jnp.broadcast_to(jnp.concatenate([kv[:, 256:512], win], axis=1)[:, None, :], (dec_b, 8, 512))
        o16 = _nsa_sample_attn(ids_s, idx[:, :N_KV, :N_SELECT].reshape(dec_b, N_KV * N_SELECT), cache, q16, ocmp,
                               gate16, win_state, new_rows, l, past)
        attn = jnp.concatenate([o16[:, h, (h // Q_PER_KV) * HEAD_DIM:(h // Q_PER_KV + 1) * HEAD_DIM]
                                for h in range(N_HEADS)], axis=1)
        u_ext = jnp.concatenate([state_pool[l], u[:, None, :]], axis=1)
        py = _pool(u_ext, prm['pw'], prm['ps'], POOL_BUF + 1)[:, POOL_BUF]
        hs = _mixer_out(hs, py, gp, attn, ga, p_sample[l].reshape(dec_b, D_PLE), prm['wo'], prm['wg'], prm['wp'], dec_b)
        outs[1].append(kv.reshape(dec_b, 1, 4, N_KV, HEAD_DIM))
        outs[3].append(jnp.concatenate([state_win[l][:, 1:], win.reshape(dec_b, 1, 2, N_KV, HEAD_DIM)], axis=1))
        outs[5].append(u_ext[:, 1:])

    return (hp.reshape(bsz, seq, D_MODEL), hs.reshape(dec_b, 1, D_MODEL),
            jnp.stack(outs[0]), jnp.stack(outs[1]), jnp.stack(outs[2]), jnp.stack(outs[3]),
            jnp.stack(outs[4]), jnp.stack(outs[5]))
```

```python
import functools
import math

import jax
import jax.numpy as jnp
from jax import lax
from jax.experimental import pallas as pl
from jax.experimental.pallas import tpu as pltpu

F32 = jnp.float32
BF16 = jnp.bfloat16

D_MODEL = 1024
D_POOL = 512
POOL_WINDOWS = (2, 4, 8, 16)
POOL_GW = 128
POOL_BUF = 15
D_ATTN = 512
HEAD_DIM = 64
N_HEADS = 8
N_KV = 2
Q_PER_KV = 4
KV_W = 128
ROPE_DIM = 16
ROPE_THETA = 500000.0
CMP_LEN = 32
CMP_STRIDE = 16
SEL_LEN = 64
N_SELECT = 16
WINDOW = 512
D_PLE = 256
EPS = 1e-6
BIG = 1e30
NEG = -3.0e38
LANES = 128
D_INR = 2944
PAGE = 128
VMEM_LIMIT = 56 * 1024 * 1024
Q_SCALE = HEAD_DIM ** -0.5 * math.log2(math.e)


def _dot(a, b):
    return jnp.dot(a, b, preferred_element_type=F32)


def _dot_nt(a, b):
    return lax.dot_general(a, b, (((1,), (1,)), ((), ())), preferred_element_type=F32)


def _split_bf16(x):
    hi = x.astype(BF16)
    return hi, (x - hi.astype(F32)).astype(BF16)


def _silu(x):
    return x * jax.nn.sigmoid(x)


def _mixer_in_kernel(x_ref, cos_ref, s1_ref, s2_ref, ng_ref, w_ref, gain_ref, j_ref,
                     u_ref, gp_ref, q_ref, kv_ref, win_ref, ga_ref, gbr_ref,
                     kaug_ref, vsb_ref, kwb_ref, vwb_ref, *, tm, nper):
    x = x_ref[...]
    ms = jnp.mean(x * x, axis=-1, keepdims=True)
    xn = (x * lax.rsqrt(ms + EPS) * ng_ref[...]).astype(BF16)
    cos, s1, s2 = cos_ref[...], s1_ref[...], s2_ref[...]
    jm = j_ref[...]
    lane = lax.broadcasted_iota(jnp.int32, (tm, LANES), 1)
    lo_half = lane < HEAD_DIM

    def normrope(zc, gain):
        hi, lo = _split_bf16(zc * zc)
        msq = _dot(hi, jm) + _dot(lo, jm)
        y = zc * lax.rsqrt(msq + EPS) * gain
        return y * cos + pltpu.roll(y, 8, 1) * s1 + pltpu.roll(y, LANES - 8, 1) * s2

    u_ref[...] = _dot(xn, w_ref[:, 0:512])
    gp_ref[...] = _dot(xn, w_ref[:, 512:1024])
    zq = _dot(xn, w_ref[:, 1024:1536])
    for c in range(4):
        yc = normrope(zq[:, c * LANES:(c + 1) * LANES], gain_ref[:, c * LANES:(c + 1) * LANES])
        yr = pltpu.roll(yc, HEAD_DIM, 1)
        if c // 2 == 0:
            h0 = jnp.where(lo_half, yc, 0.0)
            h1 = jnp.where(lo_half, yr, 0.0)
        else:
            h0 = jnp.where(lo_half, 0.0, yr)
            h1 = jnp.where(lo_half, 0.0, yc)
        q_ref[:, (2 * c) * LANES:(2 * c + 1) * LANES] = h0.astype(BF16)
        q_ref[:, (2 * c + 1) * LANES:(2 * c + 2) * LANES] = h1.astype(BF16)
    zkv = _dot(xn, w_ref[:, 1536:2304])
    kc = normrope(zkv[:, 0:128], gain_ref[:, 512:640])
    vc = zkv[:, 128:256]
    ks = normrope(zkv[:, 256:384], gain_ref[:, 640:768])
    vs = zkv[:, 384:512]
    kw = normrope(zkv[:, 512:640], gain_ref[:, 768:896])
    vw = zkv[:, 640:768]
    kv_ref[:, 0:128] = kc
    kv_ref[:, 128:256] = vc
    kv_ref[:, 256:384] = ks
    kv_ref[:, 384:512] = vs
    win_ref[:, 0:128] = kw
    win_ref[:, 128:256] = vw
    ga_ref[...] = _dot(xn, w_ref[:, 2304:2816])
    gbr_ref[...] = jax.nn.sigmoid(_dot(xn, w_ref[:, 2816:2944]))
    pos = (pl.program_id(0) % nper) * tm + lax.broadcasted_iota(jnp.int32, (tm, LANES), 0)
    ones = jnp.ones((tm, LANES), BF16)
    kaug_ref[:, 0:128] = ks.astype(BF16)
    kaug_ref[:, 128:256] = jnp.where(lane == pos // SEL_LEN, 1.0, 0.0).astype(BF16)
    vsb_ref[:, 0:128] = vs.astype(BF16)
    vsb_ref[:, 128:256] = ones
    kwb_ref[...] = kw.astype(BF16)
    vwb_ref[:, 0:128] = vw.astype(BF16)
    vwb_ref[:, 128:256] = ones


def _mixer_in(x, tabs, ng, w, gain, jm, tm):
    n = x.shape[0]
    nper = tabs[0].shape[0] // tm
    row = lambda width: pl.BlockSpec((tm, width), lambda i: (i, 0))
    full = lambda a: pl.BlockSpec(a.shape, lambda i: (0,) * a.ndim)
    tab = pl.BlockSpec((tm, LANES), lambda i: (i % nper, 0))
    widths = [(512, F32), (512, F32), (1024, BF16), (512, F32), (256, F32), (512, F32), (128, F32),
              (256, BF16), (256, BF16), (128, BF16), (256, BF16)]
    return pl.pallas_call(
        functools.partial(_mixer_in_kernel, tm=tm, nper=nper),
        out_shape=[jax.ShapeDtypeStruct((n, wd), dt) for wd, dt in widths],
        grid=(n // tm,),
        in_specs=[row(D_MODEL), tab, tab, tab, full(ng), full(w), full(gain), full(jm)],
        out_specs=[row(wd) for wd, _ in widths],
        compiler_params=pltpu.CompilerParams(dimension_semantics=("arbitrary",),
                                             vmem_limit_bytes=VMEM_LIMIT),
        name="mixer_in",
    )(x, *tabs, ng, w, gain, jm)


def _compress_kernel(pt_ref, src_ref, wk_ref, wv_ref, pe_ref, w2_ref, ck_ref, cv_ref, cvt_ref,
                     *scratch, n_pages, nb, feature_major):
    if feature_major:
        stage_ref, buf_ref, bias_ref, sem_ref = scratch
    else:
        buf_ref, bias_ref, sem_ref = scratch
    b = pl.program_id(0)
    nchunk = n_pages * PAGE // CMP_STRIDE
    slot = b % 2

    def page_copies(bb, sl, j):
        if feature_major:
            return [pltpu.make_async_copy(src_ref.at[pt_ref[bb, j], pl.ds(0, 2 * KV_W), :],
                                          stage_ref.at[sl, j], sem_ref.at[sl])]
        return [pltpu.make_async_copy(src_ref.at[pt_ref[bb, j], :, pl.ds(i * KV_W, KV_W)],
                                      buf_ref.at[sl, i, pl.ds(j * PAGE, PAGE), :], sem_ref.at[sl])
                for i in range(2)]

    def start_all(bb, sl):
        def body(j, c):
            for cp in page_copies(bb, sl, j):
                cp.start()
            return c
        lax.fori_loop(0, n_pages, body, 0)

    def wait_all(bb, sl):
        def body(j, c):
            for cp in page_copies(bb, sl, j):
                cp.wait()
            return c
        lax.fori_loop(0, n_pages, body, 0)

    @pl.when(b == 0)
    def _():
        start_all(0, 0)
        acc = jnp.zeros((8, 256), F32)
        for t in range(CMP_STRIDE):
            a0 = jnp.broadcast_to(pe_ref[t:t + 1, :], (8, 256)).astype(BF16)
            a1 = jnp.broadcast_to(pe_ref[CMP_STRIDE + t:CMP_STRIDE + t + 1, :], (8, 256)).astype(BF16)
            pk0 = _dot(a0[:, 0:128], wk_ref[t])
            pk1 = _dot(a1[:, 0:128], wk_ref[t])
            pv0 = _dot(a0[:, 128:256], wv_ref[t])
            pv1 = _dot(a1[:, 128:256], wv_ref[t])
            acc = acc + jnp.concatenate([pk0[:, 0:128] + pk1[:, 128:256],
                                         pv0[:, 0:128] + pv1[:, 128:256]], axis=1)
        bias_ref[...] = acc

    @pl.when(b + 1 < nb)
    def _():
        start_all(b + 1, 1 - slot)

    wait_all(b, slot)

    if feature_major:
        def tr(j, c):
            xt = stage_ref[slot, j]
            r0 = pl.multiple_of(j * PAGE, PAGE)
            buf_ref[0, pl.ds(r0, PAGE), :] = xt[0:KV_W, :].T
            buf_ref[1, pl.ds(r0, PAGE), :] = xt[KV_W:2 * KV_W, :].T
            return c
        lax.fori_loop(0, n_pages, tr, 0)
        rows = lambda i: buf_ref.at[i]
    else:
        rows = lambda i: buf_ref.at[slot, i]

    acc_k = jnp.zeros((nchunk, 256), F32)
    acc_v = jnp.zeros((nchunk, 256), F32)
    for t in range(CMP_STRIDE):
        xk = rows(0)[pl.ds(t, nchunk, stride=CMP_STRIDE), :]
        xv = rows(1)[pl.ds(t, nchunk, stride=CMP_STRIDE), :]
        acc_k = acc_k + _dot(xk.astype(BF16), wk_ref[t])
        acc_v = acc_v + _dot(xv.astype(BF16), wv_ref[t])
    row = lax.broadcasted_iota(jnp.int32, (nchunk, LANES), 0)
    valid = row < nchunk - 1
    res = []
    for i, acc in enumerate((acc_k, acc_v)):
        pre = acc[:, 0:128] + pltpu.roll(acc[:, 128:256], nchunk - 1, 0) + bias_ref[0:1, i * 128:(i + 1) * 128]
        hid = _silu(pre).astype(BF16)
        res.append(jnp.where(valid, _dot(hid, w2_ref[i]), 0.0))
    ck_ref[0] = res[0].astype(BF16)
    cv_ref[0] = res[1].astype(BF16)
    cvt_ref[0] = res[1].T.astype(BF16)


def _compress(page_ids, src, wk, wv, pe2, w2, feature_major):
    nb, n_pages = page_ids.shape
    nchunk = n_pages * PAGE // CMP_STRIDE
    full = lambda a: pl.BlockSpec(a.shape, lambda b, pt: (0,) * a.ndim)
    if feature_major:
        scratch = [pltpu.VMEM((2, n_pages, 2 * KV_W, PAGE), F32), pltpu.VMEM((2, n_pages * PAGE, KV_W), F32)]
    else:
        scratch = [pltpu.VMEM((2, 2, n_pages * PAGE, KV_W), F32)]
    return pl.pallas_call(
        functools.partial(_compress_kernel, n_pages=n_pages, nb=nb, feature_major=feature_major),
        out_shape=[jax.ShapeDtypeStruct((nb, nchunk, KV_W), BF16), jax.ShapeDtypeStruct((nb, nchunk, KV_W), BF16),
                   jax.ShapeDtypeStruct((nb, KV_W, nchunk), BF16)],
        grid_spec=pltpu.PrefetchScalarGridSpec(
            num_scalar_prefetch=1,
            grid=(nb,),
            in_specs=[pl.BlockSpec(memory_space=pl.ANY), full(wk), full(wv), full(pe2), full(w2)],
            out_specs=[pl.BlockSpec((1, nchunk, KV_W), lambda b, pt: (b, 0, 0)),
                       pl.BlockSpec((1, nchunk, KV_W), lambda b, pt: (b, 0, 0)),
                       pl.BlockSpec((1, KV_W, nchunk), lambda b, pt: (b, 0, 0))],
            scratch_shapes=scratch + [pltpu.VMEM((8, 256), F32), pltpu.SemaphoreType.DMA((2,))],
        ),
        compiler_params=pltpu.CompilerParams(dimension_semantics=("arbitrary",),
                                             vmem_limit_bytes=VMEM_LIMIT),
        name="compress",
    )(page_ids, src, wk, wv, pe2, w2)


def _select_topk(x, blk, axis, on_pick):
    nl = float(x.shape[axis])
    blkf = blk.astype(F32)
    for it in range(N_SELECT):
        m = jnp.max(x, axis=axis, keepdims=True)
        idx = jnp.min(jnp.where(x == m, blkf, nl), axis=axis, keepdims=True)
        chosen = blkf == idx
        on_pick(it, idx, chosen, m > -0.5 * BIG)
        x = jnp.where(chosen, NEG, x)


def _importance(imp, blk, cur):
    forced = (blk == 0) | (blk == cur) | (blk == cur - 1)
    return jnp.where(forced, BIG, jnp.where(blk <= cur, imp, -BIG))


def _nsa_prompt_kernel(q_ref, gbr_ref, ck_ref, cvt_ref, kaug_ref, vs_ref, kw_ref, vw_ref, c2st_ref,
                       o_ref, lhs_ref, *, tq, tk):
    t0 = pl.program_id(1) * tq
    rows = Q_PER_KV * tq
    nc = ck_ref.shape[1]
    qpos_r = t0 + lax.broadcasted_iota(jnp.int32, (rows, 1), 0) % tq
    qpos_c = t0 + lax.broadcasted_iota(jnp.int32, (1, rows), 1) % tq
    qpos_t = t0 + lax.broadcasted_iota(jnp.int32, (1, tq), 1)
    blk_t = lax.broadcasted_iota(jnp.int32, (LANES, tq), 0)
    cend = lax.broadcasted_iota(jnp.int32, (nc, 1), 0) * CMP_STRIDE + (CMP_LEN - 1)
    lo_half = lax.broadcasted_iota(jnp.int32, (tq, LANES), 1) < HEAD_DIM
    nwin = WINDOW + tq
    kstart = pl.multiple_of(jnp.maximum(t0 - WINDOW, 0), tq)
    cmask = cend <= qpos_c

    o_cmp, o_win = [], []
    for g in range(N_KV):
        for r in range(Q_PER_KV):
            h = Q_PER_KV * g + r
            lhs_ref[g, r * tq:(r + 1) * tq, 0:LANES] = q_ref[0, :, h * LANES:(h + 1) * LANES]
        qg = lhs_ref[g, :, 0:LANES]

        s = jnp.where(cmask, _dot_nt(ck_ref[0], qg), -BIG)
        e = jnp.where(cmask, jnp.exp2(s - jnp.max(s, axis=0, keepdims=True)), 0.0)
        l = jnp.sum(e, axis=0, keepdims=True)
        p = e * jnp.where(l > 0.0, 1.0 / l, 0.0)
        o_cmp.append(_dot(cvt_ref[0], p.astype(BF16)).T)
        psum = p[:, 0:tq] + p[:, tq:2 * tq] + p[:, 2 * tq:3 * tq] + p[:, 3 * tq:4 * tq]
        hi, lo = _split_bf16(psum)
        imp = _dot(c2st_ref[...], hi) + _dot(c2st_ref[...], lo)

        sel = [jnp.zeros((LANES, tq), jnp.bool_)]

        def on_pick(it, idx, chosen, valid):
            sel[0] = sel[0] | (chosen & valid)

        _select_topk(_importance(imp, blk_t, qpos_t // SEL_LEN), blk_t, 0, on_pick)
        selbias = jnp.where(sel[0], 0.0, -BIG).T.astype(BF16)
        for r in range(Q_PER_KV):
            lhs_ref[g, r * tq:(r + 1) * tq, LANES:2 * LANES] = selbias

        kwpos = kstart + lax.broadcasted_iota(jnp.int32, (1, nwin), 1)
        wmask = (kwpos <= qpos_r) & (kwpos >= qpos_r - WINDOW)
        s = jnp.where(wmask, _dot_nt(qg, kw_ref[0, pl.ds(kstart, nwin), :]), -BIG)
        e = jnp.exp2(s - jnp.max(s, axis=1, keepdims=True))
        ow = _dot(e.astype(BF16), vw_ref[0, pl.ds(kstart, nwin), :])
        o_win.append(ow[:, 0:LANES] / ow[:, LANES:2 * LANES])

    def sel_tile(kt, carry):
        k0 = pl.multiple_of(kt * tk, tk)
        ks = kaug_ref[0, pl.ds(k0, tk), :]
        vsa = vs_ref[0, pl.ds(k0, tk), :]
        causal = (k0 + lax.broadcasted_iota(jnp.int32, (1, tk), 1)) <= qpos_r
        new = []
        for g in range(N_KV):
            m, acc = carry[g]
            s_ = jnp.where(causal, _dot_nt(lhs_ref[g], ks), -BIG)
            m_new = jnp.maximum(m, jnp.max(s_, axis=1, keepdims=True))
            p_ = jnp.exp2(s_ - m_new)
            acc = jnp.exp2(m - m_new) * acc + _dot(p_.astype(BF16), vsa)
            new.append((m_new, acc))
        return tuple(new)

    init = tuple((jnp.full((rows, 1), NEG, F32), jnp.zeros((rows, 2 * LANES), F32)) for _ in range(N_KV))
    res = lax.fori_loop(0, (t0 + tq + tk - 1) // tk, sel_tile, init)

    for g in range(N_KV):
        acc = res[g][1]
        o_sel = acc[:, 0:LANES] / acc[:, LANES:2 * LANES]
        pieces = []
        for r in range(Q_PER_KV):
            h = Q_PER_KV * g + r
            sl = slice(r * tq, (r + 1) * tq)
            oh = (o_cmp[g][sl] * gbr_ref[0, :, 3 * h:3 * h + 1] + o_sel[sl] * gbr_ref[0, :, 3 * h + 1:3 * h + 2]
                  + o_win[g][sl] * gbr_ref[0, :, 3 * h + 2:3 * h + 3])
            pieces.append(oh if (h % 2) == g else pltpu.roll(oh, HEAD_DIM, 1))
        for c in range(2):
            o_ref[0, :, (2 * g + c) * LANES:(2 * g + c + 1) * LANES] = jnp.where(lo_half, pieces[2 * c], pieces[2 * c + 1])


def _nsa_prompt(qpad, gbr, ck, cvt, kaug, vsb, kwb, vwb, c2st, tq, tk):
    bsz, seq = qpad.shape[:2]
    per_b = lambda a: pl.BlockSpec((1,) + a.shape[1:], lambda b, i: (b, 0, 0))
    return pl.pallas_call(
        functools.partial(_nsa_prompt_kernel, tq=tq, tk=tk),
        out_shape=jax.ShapeDtypeStruct((bsz, seq, D_ATTN), F32),
        grid=(bsz, seq // tq),
        in_specs=[pl.BlockSpec((1, tq, N_HEADS * LANES), lambda b, i: (b, i, 0)),
                  pl.BlockSpec((1, tq, LANES), lambda b, i: (b, i, 0)),
                  per_b(ck), per_b(cvt), per_b(kaug), per_b(vsb), per_b(kwb), per_b(vwb),
                  pl.BlockSpec(c2st.shape, lambda b, i: (0, 0))],
        out_specs=pl.BlockSpec((1, tq, D_ATTN), lambda b, i: (b, i, 0)),
        scratch_shapes=[pltpu.VMEM((N_KV, Q_PER_KV * tq, 2 * LANES), BF16)],
        compiler_params=pltpu.CompilerParams(dimension_semantics=("arbitrary", "arbitrary"),
                                             vmem_limit_bytes=VMEM_LIMIT),
        name="nsa_prompt",
    )(qpad, gbr, ck, cvt, kaug, vsb, kwb, vwb, c2st)


def _nsa_sample_cmp_kernel(q_ref, ck_ref, cv_ref, c2s_ref, ocmp_ref, idx_ref, *, past):
    nc = ck_ref.shape[1]
    nl = c2s_ref.shape[1]
    q = q_ref[0]
    cend = lax.broadcasted_iota(jnp.int32, (1, nc), 1) * CMP_STRIDE + (CMP_LEN - 1)
    cmask = cend <= past
    s = jnp.where(cmask, _dot_nt(q, ck_ref[0]), -BIG)
    e = jnp.where(cmask, jnp.exp2(s - jnp.max(s, axis=1, keepdims=True)), 0.0)
    l = jnp.sum(e, axis=1, keepdims=True)
    p = e * jnp.where(l > 0.0, 1.0 / l, 0.0)
    ocmp_ref[0] = _dot(p.astype(BF16), cv_ref[0])
    row = lax.broadcasted_iota(jnp.int32, (8, 1), 0)
    ps0 = jnp.sum(p[0:Q_PER_KV], axis=0, keepdims=True)
    ps1 = jnp.sum(p[Q_PER_KV:2 * Q_PER_KV], axis=0, keepdims=True)
    psum = jnp.where(row == 0, ps0, jnp.where(row == 1, ps1, 0.0))
    hi, lo = _split_bf16(psum)
    imp = _dot(hi, c2s_ref[...]) + _dot(lo, c2s_ref[...])
    blk = lax.broadcasted_iota(jnp.int32, (8, nl), 1)
    out_lane = lax.broadcasted_iota(jnp.int32, (8, LANES), 1)
    picks = [jnp.full((8, LANES), -1, jnp.int32)]

    def on_pick(it, idx, chosen, valid):
        picks[0] = jnp.where(out_lane == it, jnp.where(valid, idx.astype(jnp.int32), -1), picks[0])

    _select_topk(_importance(imp, blk, past // SEL_LEN), blk, 1, on_pick)
    idx_ref[0] = picks[0]


def _nsa_sample_cmp(q16, ck, cv, c2s, past):
    bsz = q16.shape[0]
    nc = ck.shape[1]
    return pl.pallas_call(
        functools.partial(_nsa_sample_cmp_kernel, past=past),
        out_shape=[jax.ShapeDtypeStruct((bsz, 16, LANES), F32), jax.ShapeDtypeStruct((bsz, 8, LANES), jnp.int32)],
        grid=(bsz,),
        in_specs=[pl.BlockSpec((1, 16, LANES), lambda b: (b, 0, 0)),
                  pl.BlockSpec((1, nc, KV_W), lambda b: (b, 0, 0)),
                  pl.BlockSpec((1, nc, KV_W), lambda b: (b, 0, 0)),
                  pl.BlockSpec(c2s.shape, lambda b: (0, 0))],
        out_specs=[pl.BlockSpec((1, 16, LANES), lambda b: (b, 0, 0)),
                   pl.BlockSpec((1, 8, LANES), lambda b: (b, 0, 0))],
        compiler_params=pltpu.CompilerParams(dimension_semantics=("arbitrary",)),
        name="nsa_sample_cmp",
    )(q16, ck, cv, c2s)


def _nsa_sample_attn_kernel(pt_ref, idx_ref, cache_ref, q_ref, ocmp_ref, gate_ref, win_ref, new_ref,
                            o_ref, buf_ref, sem_ref, *, past, nb):
    b = pl.program_id(0)
    nblk = N_KV * N_SELECT
    npast_blk = past // SEL_LEN
    per_page = PAGE // SEL_LEN
    slot = b % 2

    def blk_copy(bb, sl, i):
        j = jnp.clip(idx_ref[bb, i], 0, npast_blk - 1)
        return pltpu.make_async_copy(
            cache_ref.at[pt_ref[bb, j // per_page], pl.ds(2 * KV_W, 2 * KV_W), :],
            buf_ref.at[sl, i // N_SELECT, :, pl.ds(pl.multiple_of((i % N_SELECT) * PAGE, PAGE), PAGE)],
            sem_ref.at[sl])

    def start_all(bb, sl):
        def body(i, c):
            blk_copy(bb, sl, i).start()
            return c
        lax.fori_loop(0, nblk, body, 0)

    def wait_all(bb, sl):
        def body(i, c):
            blk_copy(bb, sl, i).wait()
            return c
        lax.fori_loop(0, nblk, body, 0)

    @pl.when(b == 0)
    def _():
        start_all(0, 0)

    @pl.when(b + 1 < nb)
    def _():
        start_all(b + 1, 1 - slot)

    q = q_ref[0]
    qf = q.astype(F32)
    knew = new_ref[0, 0:1, :]
    ks_new = knew[:, 0:128].astype(BF16).astype(F32)
    vs_new = knew[:, 128:256].astype(BF16).astype(F32)
    kw_new = knew[:, 256:384].astype(BF16).astype(F32)
    vw_new = knew[:, 384:512].astype(BF16).astype(F32)
    row = lax.broadcasted_iota(jnp.int32, (16, 1), 0)

    s = _dot(q, win_ref[0, 0:KV_W, :].astype(BF16))
    s_new = jnp.sum(qf * kw_new, axis=1, keepdims=True)
    m = jnp.maximum(jnp.max(s, axis=1, keepdims=True), s_new)
    e = jnp.exp2(s - m)
    e_new = jnp.exp2(s_new - m)
    o_win = ((_dot_nt(e.astype(BF16), win_ref[0, KV_W:2 * KV_W, :].astype(BF16))
              + e_new.astype(BF16).astype(F32) * vw_new) / (jnp.sum(e, axis=1, keepdims=True) + e_new))

    wait_all(b, slot)

    nkeys = N_SELECT * PAGE
    lane = lax.broadcasted_iota(jnp.int32, (1, nkeys), 1)
    s_new = jnp.sum(qf * ks_new, axis=1, keepdims=True)
    o_sel = jnp.zeros((16, LANES), F32)
    for g in range(N_KV):
        kpos = jnp.full((1, nkeys), past + 1, jnp.int32)
        for n in range(N_SELECT):
            j = idx_ref[b, g * N_SELECT + n]
            ok = (j >= 0) & (j < npast_blk)
            in_blk = (lane // PAGE == n) & ((lane % PAGE) // SEL_LEN == j % per_page) & ok
            kpos = jnp.where(in_blk, (j // per_page) * PAGE + lane % PAGE, kpos)
        s = jnp.where(kpos <= past, _dot(q, buf_ref[slot, g, 0:KV_W, :].astype(BF16)), -BIG)
        m = jnp.maximum(jnp.max(s, axis=1, keepdims=True), s_new)
        e = jnp.exp2(s - m)
        e_new = jnp.exp2(s_new - m)
        og = ((_dot_nt(e.astype(BF16), buf_ref[slot, g, KV_W:2 * KV_W, :].astype(BF16))
               + e_new.astype(BF16).astype(F32) * vs_new) / (jnp.sum(e, axis=1, keepdims=True) + e_new))
        o_sel = jnp.where((row >= g * Q_PER_KV) & (row < (g + 1) * Q_PER_KV), og, o_sel)

    gt = gate_ref[0]
    o_ref[0] = ocmp_ref[0] * gt[:, 0:1] + o_sel * gt[:, 1:2] + o_win * gt[:, 2:3]


def _nsa_sample_attn(page_ids, idx, cache_t, q16, ocmp, gate16, win_t, new_rows, layer, past):
    bsz = q16.shape[0]
    nwin = win_t.shape[3]
    return pl.pallas_call(
        functools.partial(_nsa_sample_attn_kernel, past=past, nb=bsz),
        out_shape=jax.ShapeDtypeStruct((bsz, 16, LANES), F32),
        grid_spec=pltpu.PrefetchScalarGridSpec(
            num_scalar_prefetch=2,
            grid=(bsz,),
            in_specs=[pl.BlockSpec(memory_space=pl.ANY),
                      pl.BlockSpec((1, 16, LANES), lambda b, pt, ix: (b, 0, 0)),
                      pl.BlockSpec((1, 16, LANES), lambda b, pt, ix: (b, 0, 0)),
                      pl.BlockSpec((1, 16, LANES), lambda b, pt, ix: (b, 0, 0)),
                      pl.BlockSpec((None, 1, 2 * KV_W, nwin), lambda b, pt, ix: (layer, b, 0, 0)),
                      pl.BlockSpec((1, 8, 512), lambda b, pt, ix: (b, 0, 0))],
            out_specs=pl.BlockSpec((1, 16, LANES), lambda b, pt, ix: (b, 0, 0)),
            scratch_shapes=[pltpu.VMEM((2, N_KV, 2 * KV_W, N_SELECT * PAGE), F32),
                            pltpu.SemaphoreType.DMA((2,))],
        ),
        compiler_params=pltpu.CompilerParams(dimension_semantics=("arbitrary",)),
        name="nsa_sample_attn",
    )(page_ids, idx, cache_t, q16, ocmp, gate16, win_t, new_rows)


def _pool_kernel(u_ref, halo_ref, pw_ref, ps_ref, y_ref, ext_ref, *, tm):
    i = pl.program_id(1)
    hb = POOL_BUF + 1
    ext_ref[0:hb, :] = jnp.where(i > 0, halo_ref[0], 0.0)
    ext_ref[hb:hb + tm, :] = u_ref[0]
    r = i * tm + lax.broadcasted_iota(jnp.int32, (tm, 1), 0)
    for g, w in enumerate(POOL_WINDOWS):
        cols = slice(g * POOL_GW, (g + 1) * POOL_GW)
        acc = ext_ref[hb:hb + tm, cols]
        for k in range(1, w):
            acc = acc + ext_ref[hb - k:hb - k + tm, cols]
        cnt = jnp.minimum(r + 1, w).astype(F32)
        d = acc / cnt - ext_ref[hb:hb + tm, cols]
        y_ref[0, :, cols] = _dot(d.astype(BF16), pw_ref[g]) * ps_ref[:, cols]


def _pool(u, pw, ps, tm):
    bsz, seq = u.shape[:2]
    hb = POOL_BUF + 1
    ratio = tm // hb
    return pl.pallas_call(
        functools.partial(_pool_kernel, tm=tm),
        out_shape=jax.ShapeDtypeStruct(u.shape, F32),
        grid=(bsz, seq // tm),
        in_specs=[pl.BlockSpec((1, tm, D_POOL), lambda b, i: (b, i, 0)),
                  pl.BlockSpec((1, hb, D_POOL), lambda b, i: (b, jnp.maximum(i * ratio - 1, 0), 0)),
                  pl.BlockSpec(pw.shape, lambda b, i: (0, 0, 0)),
                  pl.BlockSpec(ps.shape, lambda b, i: (0, 0))],
        out_specs=pl.BlockSpec((1, tm, D_POOL), lambda b, i: (b, i, 0)),
        scratch_shapes=[pltpu.VMEM((hb + tm, D_POOL), F32)],
        compiler_params=pltpu.CompilerParams(dimension_semantics=("arbitrary", "arbitrary")),
        name="pool",
    )(u, u, pw, ps)


def _mixer_out_kernel(h_ref, py_ref, gp_ref, at_ref, ga_ref, p_ref, wo_ref, wg_ref, wp_ref, o_ref):
    mp = (py_ref[...] * _silu(gp_ref[...])).astype(BF16)
    ma = (at_ref[...] * _silu(ga_ref[...])).astype(BF16)
    h1 = h_ref[...] + _dot(mp, wo_ref[0:D_POOL, :]) + _dot(ma, wo_ref[D_POOL:D_POOL + D_ATTN, :])
    gate = jax.nn.sigmoid(_dot(h1.astype(BF16), wg_ref[...]))
    o_ref[...] = h1 + gate * _dot(p_ref[...].astype(BF16), wp_ref[...])


def _mixer_out(h, py, gp, at, ga, p, wo, wg, wp, tm):
    n = h.shape[0]
    row = lambda a: pl.BlockSpec((tm, a.shape[1]), lambda i: (i, 0))
    full = lambda a: pl.BlockSpec(a.shape, lambda i: (0, 0))
    return pl.pallas_call(
        _mixer_out_kernel,
        out_shape=jax.ShapeDtypeStruct(h.shape, F32),
        grid=(n // tm,),
        in_specs=[row(h), row(py), row(gp), row(at), row(ga), row(p), full(wo), full(wg), full(wp)],
        out_specs=pl.BlockSpec((tm, D_MODEL), lambda i: (i, 0)),
        compiler_params=pltpu.CompilerParams(dimension_semantics=("arbitrary",),
                                             vmem_limit_bytes=VMEM_LIMIT),
        name="mixer_out",
    )(h, py, gp, at, ga, p, wo, wg, wp)


def _rope_tables(pos):
    half = ROPE_DIM // 2
    inv = ROPE_THETA ** (-jnp.arange(0, ROPE_DIM, 2, dtype=F32) / ROPE_DIM)
    ang = pos.astype(F32)[:, None] * inv[None, :]
    cos, sin = jnp.cos(ang), jnp.sin(ang)
    n = pos.shape[0]
    rest = HEAD_DIM - ROPE_DIM
    c = jnp.concatenate([cos, cos, jnp.ones((n, rest), F32)], axis=1)
    s1 = jnp.concatenate([jnp.zeros((n, half), F32), sin, jnp.zeros((n, rest), F32)], axis=1)
    s2 = jnp.concatenate([-sin, jnp.zeros((n, half + rest), F32)], axis=1)
    return tuple(jnp.tile(t, (1, LANES // HEAD_DIM)) for t in (c, s1, s2))


def _cmp_to_sel(nc, n_cmp, nl):
    ci = jnp.arange(nc)[:, None] * CMP_STRIDE
    sj = jnp.arange(nl)[None, :] * SEL_LEN
    m = (ci < sj + SEL_LEN) & (ci + CMP_LEN > sj) & (jnp.arange(nc)[:, None] < n_cmp)
    return m.astype(BF16)


def _layer_params(l, norm_g, w_in, q_norm, k_norm, cmp_pe, cmp_w1, cmp_w2, pool_w, pool_scale, w_out,
                  ple_gate, ple_proj):
    w = w_in[l]
    pad = jnp.zeros((D_MODEL, D_INR - 2840), w.dtype)
    w_r = jnp.concatenate([w[:, 0:2304], w[:, 2328:2840], w[:, 2304:2328], pad], axis=1).astype(BF16)
    ones = jnp.ones((KV_W,), F32)
    gain = jnp.concatenate([jnp.tile(q_norm[l], N_HEADS) * Q_SCALE,
                            jnp.tile(k_norm[l, 0], N_KV), jnp.tile(k_norm[l, 1], N_KV),
                            jnp.tile(k_norm[l, 2], N_KV), ones, ones, ones])[None, :].astype(F32)
    eye = jnp.eye(N_KV, dtype=F32)

    def taps(w1):
        w1r = w1.reshape(2, CMP_STRIDE, HEAD_DIM, HEAD_DIM)
        return jnp.einsum('gh,atde->tgdahe', eye, w1r).reshape(CMP_STRIDE, KV_W, 2 * KV_W).astype(BF16)

    w2 = jnp.stack([jnp.kron(eye, cmp_w2[l, i]) for i in range(2)]).astype(BF16)
    pe2 = jnp.concatenate([jnp.tile(cmp_pe[l, 0], (1, N_KV)), jnp.tile(cmp_pe[l, 1], (1, N_KV))], axis=1)
    return dict(ng=norm_g[l][None, :], w=w_r, gain=gain, wk=taps(cmp_w1[l, 0]), wv=taps(cmp_w1[l, 1]),
                w2=w2, pe2=pe2, pw=pool_w[l].astype(BF16), ps=pool_scale[l][None, :],
                wo=w_out[l].astype(BF16), wg=ple_gate[l].astype(BF16), wp=ple_proj[l].astype(BF16))


def kernel(x_prompt, x_sample, cache_kv, state_win, state_pool, page_table, p_prompt, p_sample,
           norm_g, w_in, q_norm, k_norm, cmp_pe, cmp_w1, cmp_w2, pool_w, pool_scale, w_out,
           ple_gate, ple_proj):
    depth = w_in.shape[0]
    bsz, seq, _ = x_prompt.shape
    dec_b, n_pages = page_table.shape
    n_pool, page_size = cache_kv.shape[1:3]
    past = n_pages * page_size
    n_win = state_win.shape[2]
    assert page_size == PAGE and seq % 512 == 0 and seq >= WINDOW + 128 and past >= WINDOW
    assert x_sample.shape[1] == 1 and past % SEL_LEN == 0 and n_win % LANES == 0
    n_sel_p = seq // SEL_LEN
    n_sel_s = past // SEL_LEN + 1
    assert N_SELECT <= n_sel_p <= LANES
    nl_s = -(-n_sel_s // LANES) * LANES
    nc_p, nc_s = seq // CMP_STRIDE, past // CMP_STRIDE

    tm = 256
    tabs_p = _rope_tables(jnp.arange(seq))
    tabs_s = _rope_tables(jnp.full((dec_b,), past, jnp.int32))
    jm = jnp.kron(jnp.eye(LANES // HEAD_DIM, dtype=F32), jnp.full((HEAD_DIM, HEAD_DIM), 1.0 / HEAD_DIM, F32)).astype(BF16)
    c2st_p = _cmp_to_sel(nc_p, nc_p - 1, LANES).T
    c2s_s = _cmp_to_sel(nc_s, nc_s - 1, nl_s)
    ids_p = jnp.arange(bsz * seq // PAGE, dtype=jnp.int32).reshape(bsz, seq // PAGE)
    cache_t = jnp.transpose(cache_kv, (0, 1, 3, 4, 5, 2)).reshape(depth * n_pool, 4 * KV_W, PAGE)
    win_t = jnp.transpose(state_win, (0, 1, 3, 4, 5, 2)).reshape(depth, dec_b, 2 * KV_W, n_win)

    hp = x_prompt.reshape(bsz * seq, D_MODEL)
    hs = x_sample.reshape(dec_b, D_MODEL)
    outs = [[] for _ in range(6)]
    for l in range(depth):
        prm = _layer_params(l, norm_g, w_in, q_norm, k_norm, cmp_pe, cmp_w1, cmp_w2, pool_w, pool_scale,
                            w_out, ple_gate, ple_proj)
        u, gp, qpad, kv, win, ga, gbr, kaug, vsb, kwb, vwb = _mixer_in(
            hp, tabs_p, prm['ng'], prm['w'], prm['gain'], jm, tm)
        ck, _, cvt = _compress(ids_p, kv.reshape(bsz * seq // PAGE, PAGE, 512), prm['wk'], prm['wv'], prm['pe2'],
                               prm['w2'], False)
        r3 = lambda a: a.reshape(bsz, seq, a.shape[-1])
        attn = _nsa_prompt(r3(qpad), r3(gbr), ck, cvt, r3(kaug), r3(vsb), r3(kwb), r3(vwb), c2st_p, 128, 512)
        py = _pool(r3(u), prm['pw'], prm['ps'], tm)
        hp = _mixer_out(hp, py.reshape(bsz * seq, D_POOL), gp, attn.reshape(bsz * seq, D_ATTN), ga,
                        p_prompt[l].reshape(bsz * seq, D_PLE), prm['wo'], prm['wg'], prm['wp'], tm)
        outs[0].append(kv.reshape(bsz, seq, 4, N_KV, HEAD_DIM))
        outs[2].append(win.reshape(bsz, seq, 2, N_KV, HEAD_DIM)[:, seq - min(WINDOW, seq):])
        outs[4].append(r3(u)[:, seq - POOL_BUF:])

        u, gp, qpad, kv, win, ga, gbr, _, _, _, _ = _mixer_in(
            hs, tabs_s, prm['ng'], prm['w'], prm['gain'], jm, dec_b)
        ids_s = page_table + l * n_pool
        ck, cv, _ = _compress(ids_s, cache_t, prm['wk'], prm['wv'], prm['pe2'], prm['w2'], True)
        q16 = jnp.pad(qpad.reshape(dec_b, N_HEADS, LANES), ((0, 0), (0, 16 - N_HEADS), (0, 0)))
        ocmp, idx = _nsa_sample_cmp(q16, ck, cv, c2s_s, past)
        gate16 = jnp.pad(gbr[:, :3 * N_HEADS].reshape(dec_b, N_HEADS, 3), ((0, 0), (0, 16 - N_HEADS), (0, LANES - 3)))
        new_rows = jnp.broadcast_to(jnp.concatenate([kv[:, 256:512], win], axis=1)[:, None, :], (dec_b, 8, 512))
        o16 = _nsa_sample_attn(ids_s, idx[:, :N_KV, :N_SELECT].reshape(dec_b, N_KV * N_SELECT), cache_t, q16, ocmp,
                               gate16, win_t, new_rows, l, past)
        attn = jnp.concatenate([o16[:, h, (h // Q_PER_KV) * HEAD_DIM:(h // Q_PER_KV + 1) * HEAD_DIM]
                                for h in range(N_HEADS)], axis=1)
        u_ext = jnp.concatenate([state_pool[l], u[:, None, :]], axis=1)
        py = _pool(u_ext, prm['pw'], prm['ps'], POOL_BUF + 1)[:, POOL_BUF]
        hs = _mixer_out(hs, py, gp, attn, ga, p_sample[l].reshape(dec_b, D_PLE), prm['wo'], prm['wg'], prm['wp'], dec_b)
        outs[1].append(kv.reshape(dec_b, 1, 4, N_KV, HEAD_DIM))
        outs[3].append(jnp.concatenate([state_win[l][:, 1:], win.reshape(dec_b, 1, 2, N_KV, HEAD_DIM)], axis=1))
        outs[5].append(u_ext[:, 1:])

    return (hp.reshape(bsz, seq, D_MODEL), hs.reshape(dec_b, 1, D_MODEL),
            jnp.stack(outs[0]), jnp.stack(outs[1]), jnp.stack(outs[2]), jnp.stack(outs[3]),
            jnp.stack(outs[4]), jnp.stack(outs[5]))
```

```python
import functools
import math

import jax
import jax.numpy as jnp
from jax import lax
from jax.experimental import pallas as pl
from jax.experimental.pallas import tpu as pltpu

F32 = jnp.float32
BF16 = jnp.bfloat16

D_MODEL = 1024
D_POOL = 512
POOL_WINDOWS = (2, 4, 8, 16)
POOL_GW = 128
POOL_BUF = 15
D_ATTN = 512
HEAD_DIM = 64
N_HEADS = 8
N_KV = 2
Q_PER_KV = 4
KV_W = 128
ROPE_DIM = 16
ROPE_THETA = 500000.0
CMP_LEN = 32
CMP_STRIDE = 16
SEL_LEN = 64
N_SELECT = 16
WINDOW = 512
D_PLE = 256
EPS = 1e-6
BIG = 1e30
NEG = -3.0e38
LANES = 128
D_INR = 2944
PAGE = 128
VMEM_LIMIT = 56 * 1024 * 1024
Q_SCALE = HEAD_DIM ** -0.5 * math.log2(math.e)
SHIFT_SLACK = 60.0


def _dot(a, b):
    return jnp.dot(a, b, preferred_element_type=F32)


def _dot_nt(a, b):
    return lax.dot_general(a, b, (((1,), (1,)), ((), ())), preferred_element_type=F32)


def _split_bf16(x):
    hi = x.astype(BF16)
    return hi, (x - hi.astype(F32)).astype(BF16)


def _silu(x):
    return x * jax.nn.sigmoid(x)


def _mixer_in_kernel(x_ref, cos_ref, s1_ref, s2_ref, ng_ref, w_ref, gain_ref, j_ref,
                     u_ref, gp_ref, q_ref, kv_ref, win_ref, ga_ref, gbr_ref,
                     kaug0_ref, kaug1_ref, vsb_ref, kwb_ref, vwb_ref, *, tm, nper):
    x = x_ref[...]
    ms = jnp.mean(x * x, axis=-1, keepdims=True)
    xn = (x * lax.rsqrt(ms + EPS) * ng_ref[...]).astype(BF16)
    cos, s1, s2 = cos_ref[...], s1_ref[...], s2_ref[...]
    jm = j_ref[...]
    lane = lax.broadcasted_iota(jnp.int32, (tm, LANES), 1)
    lo_half = lane < HEAD_DIM

    def normrope(zc, gain):
        hi, lo = _split_bf16(zc * zc)
        msq = _dot(hi, jm) + _dot(lo, jm)
        y = zc * lax.rsqrt(msq + EPS) * gain
        return y * cos + pltpu.roll(y, 8, 1) * s1 + pltpu.roll(y, LANES - 8, 1) * s2

    u_ref[...] = _dot(xn, w_ref[:, 0:512])
    gp_ref[...] = _dot(xn, w_ref[:, 512:1024])
    zq = _dot(xn, w_ref[:, 1024:1536])
    for c in range(4):
        yc = normrope(zq[:, c * LANES:(c + 1) * LANES], gain_ref[:, c * LANES:(c + 1) * LANES])
        yr = pltpu.roll(yc, HEAD_DIM, 1)
        if c // 2 == 0:
            h0 = jnp.where(lo_half, yc, 0.0)
            h1 = jnp.where(lo_half, yr, 0.0)
        else:
            h0 = jnp.where(lo_half, 0.0, yr)
            h1 = jnp.where(lo_half, 0.0, yc)
        q_ref[:, (2 * c) * LANES:(2 * c + 1) * LANES] = h0.astype(BF16)
        q_ref[:, (2 * c + 1) * LANES:(2 * c + 2) * LANES] = h1.astype(BF16)
    zkv = _dot(xn, w_ref[:, 1536:2304])
    kc = normrope(zkv[:, 0:128], gain_ref[:, 512:640])
    vc = zkv[:, 128:256]
    ks = normrope(zkv[:, 256:384], gain_ref[:, 640:768])
    vs = zkv[:, 384:512]
    kw = normrope(zkv[:, 512:640], gain_ref[:, 768:896])
    vw = zkv[:, 640:768]
    kv_ref[:, 0:128] = kc
    kv_ref[:, 128:256] = vc
    kv_ref[:, 256:384] = ks
    kv_ref[:, 384:512] = vs
    win_ref[:, 0:128] = kw
    win_ref[:, 128:256] = vw
    ga_ref[...] = _dot(xn, w_ref[:, 2304:2816])
    gbr_ref[...] = jax.nn.sigmoid(_dot(xn, w_ref[:, 2816:2944]))
    pos = (pl.program_id(0) % nper) * tm + lax.broadcasted_iota(jnp.int32, (tm, LANES), 0)
    ones = jnp.ones((tm, LANES), BF16)
    onehot = jnp.where(lane == pos // SEL_LEN, 1.0, 0.0).astype(BF16)
    kaug0_ref[:, 0:128] = jnp.where(lo_half, ks, jnp.where(lane == HEAD_DIM, 1.0, 0.0)).astype(BF16)
    kaug0_ref[:, 128:256] = onehot
    kaug1_ref[:, 0:128] = jnp.where(lo_half, jnp.where(lane == 0, 1.0, 0.0), ks).astype(BF16)
    kaug1_ref[:, 128:256] = onehot
    vsb_ref[:, 0:128] = vs.astype(BF16)
    vsb_ref[:, 128:256] = ones
    kwb_ref[...] = kw.astype(BF16)
    vwb_ref[:, 0:128] = vw.astype(BF16)
    vwb_ref[:, 128:256] = ones


def _mixer_in(x, tabs, ng, w, gain, jm, tm):
    n = x.shape[0]
    nper = tabs[0].shape[0] // tm
    row = lambda width: pl.BlockSpec((tm, width), lambda i: (i, 0))
    full = lambda a: pl.BlockSpec(a.shape, lambda i: (0,) * a.ndim)
    tab = pl.BlockSpec((tm, LANES), lambda i: (i % nper, 0))
    widths = [(512, F32), (512, F32), (1024, BF16), (512, F32), (256, F32), (512, F32), (128, F32),
              (256, BF16), (256, BF16), (256, BF16), (128, BF16), (256, BF16)]
    return pl.pallas_call(
        functools.partial(_mixer_in_kernel, tm=tm, nper=nper),
        out_shape=[jax.ShapeDtypeStruct((n, wd), dt) for wd, dt in widths],
        grid=(n // tm,),
        in_specs=[row(D_MODEL), tab, tab, tab, full(ng), full(w), full(gain), full(jm)],
        out_specs=[row(wd) for wd, _ in widths],
        compiler_params=pltpu.CompilerParams(dimension_semantics=("arbitrary",),
                                             vmem_limit_bytes=VMEM_LIMIT),
        name="mixer_in",
    )(x, *tabs, ng, w, gain, jm)


def _compress_kernel(pt_ref, src_ref, wk_ref, wv_ref, pe_ref, w2_ref, ck_ref, cv_ref, cvt_ref,
                     *scratch, n_pages, nb, feature_major):
    if feature_major:
        stage_ref, buf_ref, bias_ref, sem_ref = scratch
    else:
        buf_ref, bias_ref, sem_ref = scratch
    b = pl.program_id(0)
    nchunk = n_pages * PAGE // CMP_STRIDE
    slot = b % 2

    def page_copies(bb, sl, j):
        if feature_major:
            return [pltpu.make_async_copy(src_ref.at[pt_ref[bb, j], pl.ds(0, 2 * KV_W), :],
                                          stage_ref.at[sl, j], sem_ref.at[sl])]
        return [pltpu.make_async_copy(src_ref.at[pt_ref[bb, j], :, pl.ds(i * KV_W, KV_W)],
                                      buf_ref.at[sl, i, pl.ds(j * PAGE, PAGE), :], sem_ref.at[sl])
                for i in range(2)]

    def start_all(bb, sl):
        def body(j, c):
            for cp in page_copies(bb, sl, j):
                cp.start()
            return c
        lax.fori_loop(0, n_pages, body, 0)

    def wait_all(bb, sl):
        def body(j, c):
            for cp in page_copies(bb, sl, j):
                cp.wait()
            return c
        lax.fori_loop(0, n_pages, body, 0)

    @pl.when(b == 0)
    def _():
        start_all(0, 0)
        acc = jnp.zeros((8, 256), F32)
        for t in range(CMP_STRIDE):
            a0 = jnp.broadcast_to(pe_ref[t:t + 1, :], (8, 256)).astype(BF16)
            a1 = jnp.broadcast_to(pe_ref[CMP_STRIDE + t:CMP_STRIDE + t + 1, :], (8, 256)).astype(BF16)
            pk0 = _dot(a0[:, 0:128], wk_ref[t])
            pk1 = _dot(a1[:, 0:128], wk_ref[t])
            pv0 = _dot(a0[:, 128:256], wv_ref[t])
            pv1 = _dot(a1[:, 128:256], wv_ref[t])
            acc = acc + jnp.concatenate([pk0[:, 0:128] + pk1[:, 128:256],
                                         pv0[:, 0:128] + pv1[:, 128:256]], axis=1)
        bias_ref[...] = acc

    @pl.when(b + 1 < nb)
    def _():
        start_all(b + 1, 1 - slot)

    wait_all(b, slot)

    if feature_major:
        def tr(j, c):
            xt = stage_ref[slot, j]
            r0 = pl.multiple_of(j * PAGE, PAGE)
            buf_ref[0, pl.ds(r0, PAGE), :] = xt[0:KV_W, :].T
            buf_ref[1, pl.ds(r0, PAGE), :] = xt[KV_W:2 * KV_W, :].T
            return c
        lax.fori_loop(0, n_pages, tr, 0)
        rows = lambda i: buf_ref.at[i]
    else:
        rows = lambda i: buf_ref.at[slot, i]

    acc_k = jnp.zeros((nchunk, 256), F32)
    acc_v = jnp.zeros((nchunk, 256), F32)
    for t in range(CMP_STRIDE):
        xk = rows(0)[pl.ds(t, nchunk, stride=CMP_STRIDE), :]
        xv = rows(1)[pl.ds(t, nchunk, stride=CMP_STRIDE), :]
        acc_k = acc_k + _dot(xk.astype(BF16), wk_ref[t])
        acc_v = acc_v + _dot(xv.astype(BF16), wv_ref[t])
    row = lax.broadcasted_iota(jnp.int32, (nchunk, LANES), 0)
    valid = row < nchunk - 1
    res = []
    for i, acc in enumerate((acc_k, acc_v)):
        pre = acc[:, 0:128] + pltpu.roll(acc[:, 128:256], nchunk - 1, 0) + bias_ref[0:1, i * 128:(i + 1) * 128]
        hid = _silu(pre).astype(BF16)
        res.append(jnp.where(valid, _dot(hid, w2_ref[i]), 0.0))
    ck_ref[0] = res[0].astype(BF16)
    cv_ref[0] = res[1].astype(BF16)
    cvt_ref[0] = res[1].T.astype(BF16)


def _compress(page_ids, src, wk, wv, pe2, w2, feature_major):
    nb, n_pages = page_ids.shape
    nchunk = n_pages * PAGE // CMP_STRIDE
    full = lambda a: pl.BlockSpec(a.shape, lambda b, pt: (0,) * a.ndim)
    if feature_major:
        scratch = [pltpu.VMEM((2, n_pages, 2 * KV_W, PAGE), F32), pltpu.VMEM((2, n_pages * PAGE, KV_W), F32)]
    else:
        scratch = [pltpu.VMEM((2, 2, n_pages * PAGE, KV_W), F32)]
    return pl.pallas_call(
        functools.partial(_compress_kernel, n_pages=n_pages, nb=nb, feature_major=feature_major),
        out_shape=[jax.ShapeDtypeStruct((nb, nchunk, KV_W), BF16), jax.ShapeDtypeStruct((nb, nchunk, KV_W), BF16),
                   jax.ShapeDtypeStruct((nb, KV_W, nchunk), BF16)],
        grid_spec=pltpu.PrefetchScalarGridSpec(
            num_scalar_prefetch=1,
            grid=(nb,),
            in_specs=[pl.BlockSpec(memory_space=pl.ANY), full(wk), full(wv), full(pe2), full(w2)],
            out_specs=[pl.BlockSpec((1, nchunk, KV_W), lambda b, pt: (b, 0, 0)),
                       pl.BlockSpec((1, nchunk, KV_W), lambda b, pt: (b, 0, 0)),
                       pl.BlockSpec((1, KV_W, nchunk), lambda b, pt: (b, 0, 0))],
            scratch_shapes=scratch + [pltpu.VMEM((8, 256), F32), pltpu.SemaphoreType.DMA((2,))],
        ),
        compiler_params=pltpu.CompilerParams(dimension_semantics=("arbitrary",),
                                             vmem_limit_bytes=VMEM_LIMIT),
        name="compress",
    )(page_ids, src, wk, wv, pe2, w2)


def _select_topk(x, blk, axis, on_pick):
    nl = float(x.shape[axis])
    blkf = blk.astype(F32)
    for it in range(N_SELECT):
        m = jnp.max(x, axis=axis, keepdims=True)
        idx = jnp.min(jnp.where(x == m, blkf, nl), axis=axis, keepdims=True)
        chosen = blkf == idx
        on_pick(it, idx, chosen, m > -0.5 * BIG)
        x = jnp.where(chosen, NEG, x)


def _importance(imp, blk, cur):
    d = cur - blk
    return jnp.where(blk == 0, BIG, jnp.where(d == 0, BIG, jnp.where(d == 1, BIG, jnp.where(d > 1, imp, -BIG))))


def _nsa_prompt_kernel(q_ref, gbr_ref, ck_ref, cvt_ref, kaug0_ref, kaug1_ref, vs_ref, kw_ref, vw_ref, c2st_ref,
                       o_ref, lhs_ref, acc_ref, bak_ref, c_ref, p0_ref, p1_ref, *, tq, tk):
    t0 = pl.program_id(1) * tq
    rows = Q_PER_KV * tq
    nc = ck_ref.shape[1]
    kaug_refs = (kaug0_ref, kaug1_ref)
    lane_r = lax.broadcasted_iota(jnp.int32, (rows, LANES), 1)
    qpos_r = t0 + lax.broadcasted_iota(jnp.int32, (rows, 1), 0) % tq
    qpos_c = t0 + lax.broadcasted_iota(jnp.int32, (1, rows), 1) % tq
    qpos_t = t0 + lax.broadcasted_iota(jnp.int32, (1, tq), 1)
    blk_t = lax.broadcasted_iota(jnp.int32, (LANES, tq), 0)
    cend = lax.broadcasted_iota(jnp.int32, (nc, 1), 0) * CMP_STRIDE + (CMP_LEN - 1)
    lo_half = lax.broadcasted_iota(jnp.int32, (tq, LANES), 1) < HEAD_DIM
    nwin = WINDOW + tq
    kstart = pl.multiple_of(jnp.maximum(t0 - WINDOW, 0), tq)
    cmask = cend <= qpos_c

    o_cmp, o_win = [], []
    for g in range(N_KV):
        for r in range(Q_PER_KV):
            h = Q_PER_KV * g + r
            lhs_ref[g, r * tq:(r + 1) * tq, 0:LANES] = q_ref[0, :, h * LANES:(h + 1) * LANES]
        qg = lhs_ref[g, :, 0:LANES]

        s = jnp.where(cmask, _dot_nt(ck_ref[0], qg), -BIG)
        e = jnp.where(cmask, jnp.exp2(s - jnp.max(s, axis=0, keepdims=True)), 0.0)
        l = jnp.sum(e, axis=0, keepdims=True)
        p = e * jnp.where(l > 0.0, 1.0 / l, 0.0)
        o_cmp.append(_dot(cvt_ref[0], p.astype(BF16)).T)
        psum = p[:, 0:tq] + p[:, tq:2 * tq] + p[:, 2 * tq:3 * tq] + p[:, 3 * tq:4 * tq]
        hi, lo = _split_bf16(psum)
        imp = _dot(c2st_ref[...], hi) + _dot(c2st_ref[...], lo)

        bias = [jnp.full((LANES, tq), -BIG, F32)]

        def on_pick(it, idx, chosen, valid):
            bias[0] = jnp.where(chosen, jnp.where(valid, 0.0, -BIG), bias[0])

        _select_topk(_importance(imp, blk_t, qpos_t // SEL_LEN), blk_t, 0, on_pick)
        selbias = bias[0].T.astype(BF16)
        for r in range(Q_PER_KV):
            lhs_ref[g, r * tq:(r + 1) * tq, LANES:2 * LANES] = selbias

        kwpos = kstart + lax.broadcasted_iota(jnp.int32, (1, nwin), 1)
        wmask = (kwpos <= qpos_r) & (kwpos >= qpos_r - WINDOW)
        s = jnp.where(wmask, _dot_nt(qg, kw_ref[0, pl.ds(kstart, nwin), :]), -BIG)
        e = jnp.exp2(s - jnp.max(s, axis=1, keepdims=True))
        ow = _dot(e.astype(BF16), vw_ref[0, pl.ds(kstart, nwin), :])
        o_win.append(ow[:, 0:LANES] / ow[:, LANES:2 * LANES])

    last_tile = kaug0_ref.shape[1] // tk - 1

    def key_rows(kt):
        return pl.ds(pl.multiple_of(jnp.minimum(kt, last_tile) * tk, tk), tk)

    def scores(g, kt):
        causal = (kt * tk + lax.broadcasted_iota(jnp.int32, (1, tk), 1)) <= qpos_r
        return jnp.where(causal, _dot_nt(lhs_ref[g], kaug_refs[g][0, key_rows(kt), :]), -BIG)

    def add_pv(g, p, kt):
        acc_ref[g] = acc_ref[g] + _dot(p, vs_ref[0, key_rows(kt), :])

    def recentre(g, kt, p_out, first):
        s_ = scores(g, kt)
        if not first:
            c_old = c_ref[g][:, 0:1]
            s_ = s_ + c_old
        mr = jnp.max(s_, axis=1, keepdims=True)
        c_new = (mr if first else jnp.maximum(c_old, mr)).astype(BF16).astype(F32)
        p_out[g] = jnp.exp2(s_ - c_new).astype(BF16)
        acc_ref[g] = jnp.zeros((rows, 2 * LANES), F32) if first else acc_ref[g] * jnp.exp2(c_old - c_new)
        c_ref[g] = jnp.broadcast_to(c_new, (rows, LANES))
        qf = lhs_ref[g, :, 0:LANES].astype(F32)
        lhs_ref[g, :, 0:LANES] = jnp.where(lane_r == HEAD_DIM * (1 - g), -c_new, qf).astype(BF16)

    for g in range(N_KV):
        recentre(g, 0, p0_ref, True)

    def fast_tile(g, kt, p_in, p_out):
        add_pv(g, p_in[g], kt - 1)
        s_ = scores(g, kt)
        p_out[g] = jnp.exp2(s_).astype(BF16)
        return jnp.max(s_)

    def sel_pair(j, carry):
        a = 2 * j + 1
        for g in range(N_KV):
            bak_ref[g] = acc_ref[g]
        top = [fast_tile(g, a, p0_ref, p1_ref) for g in range(N_KV)]
        top += [fast_tile(g, a + 1, p1_ref, p0_ref) for g in range(N_KV)]

        @pl.when(jnp.logical_not(jnp.maximum(jnp.maximum(top[0], top[1]), jnp.maximum(top[2], top[3])) <= SHIFT_SLACK))
        def _():
            for g in range(N_KV):
                acc_ref[g] = bak_ref[g]
                add_pv(g, jnp.exp2(scores(g, a - 1)).astype(BF16), a - 1)
                recentre(g, a, p1_ref, False)
                add_pv(g, p1_ref[g], a)
                recentre(g, a + 1, p0_ref, False)

        return carry

    n_pairs = ((t0 + tq + tk - 1) // tk) // 2
    lax.fori_loop(0, n_pairs, sel_pair, 0)
    for g in range(N_KV):
        add_pv(g, p0_ref[g], 2 * n_pairs)


    for g in range(N_KV):
        acc = acc_ref[g]
        o_sel = acc[:, 0:LANES] / acc[:, LANES:2 * LANES]
        pieces = []
        for r in range(Q_PER_KV):
            h = Q_PER_KV * g + r
            sl = slice(r * tq, (r + 1) * tq)
            oh = (o_cmp[g][sl] * gbr_ref[0, :, 3 * h:3 * h + 1] + o_sel[sl] * gbr_ref[0, :, 3 * h + 1:3 * h + 2]
                  + o_win[g][sl] * gbr_ref[0, :, 3 * h + 2:3 * h + 3])
            pieces.append(oh if (h % 2) == g else pltpu.roll(oh, HEAD_DIM, 1))
        for c in range(2):
            o_ref[0, :, (2 * g + c) * LANES:(2 * g + c + 1) * LANES] = jnp.where(lo_half, pieces[2 * c], pieces[2 * c + 1])


def _nsa_prompt(qpad, gbr, ck, cvt, kaug0, kaug1, vsb, kwb, vwb, c2st, tq, tk):
    bsz, seq = qpad.shape[:2]
    rows = Q_PER_KV * tq
    per_b = lambda a: pl.BlockSpec((1,) + a.shape[1:], lambda b, i: (b, 0, 0), pipeline_mode=pl.Buffered(1))
    return pl.pallas_call(
        functools.partial(_nsa_prompt_kernel, tq=tq, tk=tk),
        out_shape=jax.ShapeDtypeStruct((bsz, seq, D_ATTN), F32),
        grid=(bsz, seq // tq),
        in_specs=[pl.BlockSpec((1, tq, N_HEADS * LANES), lambda b, i: (b, i, 0)),
                  pl.BlockSpec((1, tq, LANES), lambda b, i: (b, i, 0)),
                  per_b(ck), per_b(cvt), per_b(kaug0), per_b(kaug1), per_b(vsb), per_b(kwb), per_b(vwb),
                  pl.BlockSpec(c2st.shape, lambda b, i: (0, 0))],
        out_specs=pl.BlockSpec((1, tq, D_ATTN), lambda b, i: (b, i, 0)),
        scratch_shapes=[pltpu.VMEM((N_KV, rows, 2 * LANES), BF16),
                        pltpu.VMEM((N_KV, rows, 2 * LANES), F32),
                        pltpu.VMEM((N_KV, rows, 2 * LANES), F32),
                        pltpu.VMEM((N_KV, rows, LANES), F32),
                        pltpu.VMEM((N_KV, rows, tk), BF16),
                        pltpu.VMEM((N_KV, rows, tk), BF16)],
        compiler_params=pltpu.CompilerParams(dimension_semantics=("arbitrary", "arbitrary"),
                                             vmem_limit_bytes=VMEM_LIMIT),
        name="nsa_prompt",
    )(qpad, gbr, ck, cvt, kaug0, kaug1, vsb, kwb, vwb, c2st)


def _nsa_sample_cmp_kernel(q_ref, ck_ref, cv_ref, c2s_ref, ocmp_ref, idx_ref, *, past):
    nc = ck_ref.shape[1]
    nl = c2s_ref.shape[1]
    q = q_ref[0]
    cend = lax.broadcasted_iota(jnp.int32, (1, nc), 1) * CMP_STRIDE + (CMP_LEN - 1)
    cmask = cend <= past
    s = jnp.where(cmask, _dot_nt(q, ck_ref[0]), -BIG)
    e = jnp.where(cmask, jnp.exp2(s - jnp.max(s, axis=1, keepdims=True)), 0.0)
    l = jnp.sum(e, axis=1, keepdims=True)
    p = e * jnp.where(l > 0.0, 1.0 / l, 0.0)
    ocmp_ref[0] = _dot(p.astype(BF16), cv_ref[0])
    row = lax.broadcasted_iota(jnp.int32, (8, 1), 0)
    ps0 = jnp.sum(p[0:Q_PER_KV], axis=0, keepdims=True)
    ps1 = jnp.sum(p[Q_PER_KV:2 * Q_PER_KV], axis=0, keepdims=True)
    psum = jnp.where(row == 0, ps0, jnp.where(row == 1, ps1, 0.0))
    hi, lo = _split_bf16(psum)
    imp = _dot(hi, c2s_ref[...]) + _dot(lo, c2s_ref[...])
    blk = lax.broadcasted_iota(jnp.int32, (8, nl), 1)
    out_lane = lax.broadcasted_iota(jnp.int32, (8, LANES), 1)
    picks = [jnp.full((8, LANES), -1, jnp.int32)]

    def on_pick(it, idx, chosen, valid):
        picks[0] = jnp.where(out_lane == it, jnp.where(valid, idx.astype(jnp.int32), -1), picks[0])

    _select_topk(_importance(imp, blk, past // SEL_LEN), blk, 1, on_pick)
    idx_ref[0] = picks[0]


def _nsa_sample_cmp(q16, ck, cv, c2s, past):
    bsz = q16.shape[0]
    nc = ck.shape[1]
    return pl.pallas_call(
        functools.partial(_nsa_sample_cmp_kernel, past=past),
        out_shape=[jax.ShapeDtypeStruct((bsz, 16, LANES), F32), jax.ShapeDtypeStruct((bsz, 8, LANES), jnp.int32)],
        grid=(bsz,),
        in_specs=[pl.BlockSpec((1, 16, LANES), lambda b: (b, 0, 0)),
                  pl.BlockSpec((1, nc, KV_W), lambda b: (b, 0, 0)),
                  pl.BlockSpec((1, nc, KV_W), lambda b: (b, 0, 0)),
                  pl.BlockSpec(c2s.shape, lambda b: (0, 0))],
        out_specs=[pl.BlockSpec((1, 16, LANES), lambda b: (b, 0, 0)),
                   pl.BlockSpec((1, 8, LANES), lambda b: (b, 0, 0))],
        compiler_params=pltpu.CompilerParams(dimension_semantics=("arbitrary",)),
        name="nsa_sample_cmp",
    )(q16, ck, cv, c2s)


def _nsa_sample_attn_kernel(pt_ref, idx_ref, cache_ref, q_ref, ocmp_ref, gate_ref, win_ref, new_ref,
                            o_ref, buf_ref, sem_ref, *, past, nb):
    b = pl.program_id(0)
    nblk = N_KV * N_SELECT
    npast_blk = past // SEL_LEN
    per_page = PAGE // SEL_LEN
    slot = b % 2

    def blk_copy(bb, sl, i):
        j = jnp.clip(idx_ref[bb, i], 0, npast_blk - 1)
        return pltpu.make_async_copy(
            cache_ref.at[pt_ref[bb, j // per_page], pl.ds(2 * KV_W, 2 * KV_W), :],
            buf_ref.at[sl, i // N_SELECT, :, pl.ds(pl.multiple_of((i % N_SELECT) * PAGE, PAGE), PAGE)],
            sem_ref.at[sl])

    def start_all(bb, sl):
        def body(i, c):
            blk_copy(bb, sl, i).start()
            return c
        lax.fori_loop(0, nblk, body, 0)

    def wait_all(bb, sl):
        def body(i, c):
            blk_copy(bb, sl, i).wait()
            return c
        lax.fori_loop(0, nblk, body, 0)

    @pl.when(b == 0)
    def _():
        start_all(0, 0)

    @pl.when(b + 1 < nb)
    def _():
        start_all(b + 1, 1 - slot)

    q = q_ref[0]
    qf = q.astype(F32)
    knew = new_ref[0, 0:1, :]
    ks_new = knew[:, 0:128].astype(BF16).astype(F32)
    vs_new = knew[:, 128:256].astype(BF16).astype(F32)
    kw_new = knew[:, 256:384].astype(BF16).astype(F32)
    vw_new = knew[:, 384:512].astype(BF16).astype(F32)
    row = lax.broadcasted_iota(jnp.int32, (16, 1), 0)

    s = _dot(q, win_ref[0, 0:KV_W, :].astype(BF16))
    s_new = jnp.sum(qf * kw_new, axis=1, keepdims=True)
    m = jnp.maximum(jnp.max(s, axis=1, keepdims=True), s_new)
    e = jnp.exp2(s - m)
    e_new = jnp.exp2(s_new - m)
    o_win = ((_dot_nt(e.astype(BF16), win_ref[0, KV_W:2 * KV_W, :].astype(BF16))
              + e_new.astype(BF16).astype(F32) * vw_new) / (jnp.sum(e, axis=1, keepdims=True) + e_new))

    wait_all(b, slot)

    nkeys = N_SELECT * PAGE
    lane = lax.broadcasted_iota(jnp.int32, (1, nkeys), 1)
    s_new = jnp.sum(qf * ks_new, axis=1, keepdims=True)
    o_sel = jnp.zeros((16, LANES), F32)
    for g in range(N_KV):
        kpos = jnp.full((1, nkeys), past + 1, jnp.int32)
        for n in range(N_SELECT):
            j = idx_ref[b, g * N_SELECT + n]
            ok = (j >= 0) & (j < npast_blk)
            in_blk = (lane // PAGE == n) & ((lane % PAGE) // SEL_LEN == j % per_page) & ok
            kpos = jnp.where(in_blk, (j // per_page) * PAGE + lane % PAGE, kpos)
        s = jnp.where(kpos <= past, _dot(q, buf_ref[slot, g, 0:KV_W, :].astype(BF16)), -BIG)
        m = jnp.maximum(jnp.max(s, axis=1, keepdims=True), s_new)
        e = jnp.exp2(s - m)
        e_new = jnp.exp2(s_new - m)
        og = ((_dot_nt(e.astype(BF16), buf_ref[slot, g, KV_W:2 * KV_W, :].astype(BF16))
               + e_new.astype(BF16).astype(F32) * vs_new) / (jnp.sum(e, axis=1, keepdims=True) + e_new))
        o_sel = jnp.where((row >= g * Q_PER_KV) & (row < (g + 1) * Q_PER_KV), og, o_sel)

    gt = gate_ref[0]
    o_ref[0] = ocmp_ref[0] * gt[:, 0:1] + o_sel * gt[:, 1:2] + o_win * gt[:, 2:3]


def _nsa_sample_attn(page_ids, idx, cache_t, q16, ocmp, gate16, win_t, new_rows, layer, past):
    bsz = q16.shape[0]
    nwin = win_t.shape[3]
    return pl.pallas_call(
        functools.partial(_nsa_sample_attn_kernel, past=past, nb=bsz),
        out_shape=jax.ShapeDtypeStruct((bsz, 16, LANES), F32),
        grid_spec=pltpu.PrefetchScalarGridSpec(
            num_scalar_prefetch=2,
            grid=(bsz,),
            in_specs=[pl.BlockSpec(memory_space=pl.ANY),
                      pl.BlockSpec((1, 16, LANES), lambda b, pt, ix: (b, 0, 0)),
                      pl.BlockSpec((1, 16, LANES), lambda b, pt, ix: (b, 0, 0)),
                      pl.BlockSpec((1, 16, LANES), lambda b, pt, ix: (b, 0, 0)),
                      pl.BlockSpec((None, 1, 2 * KV_W, nwin), lambda b, pt, ix: (layer, b, 0, 0)),
                      pl.BlockSpec((1, 8, 512), lambda b, pt, ix: (b, 0, 0))],
            out_specs=pl.BlockSpec((1, 16, LANES), lambda b, pt, ix: (b, 0, 0)),
            scratch_shapes=[pltpu.VMEM((2, N_KV, 2 * KV_W, N_SELECT * PAGE), F32),
                            pltpu.SemaphoreType.DMA((2,))],
        ),
        compiler_params=pltpu.CompilerParams(dimension_semantics=("arbitrary",)),
        name="nsa_sample_attn",
    )(page_ids, idx, cache_t, q16, ocmp, gate16, win_t, new_rows)


def _pool_kernel(u_ref, halo_ref, pw_ref, ps_ref, y_ref, ext_ref, *, tm):
    i = pl.program_id(1)
    hb = POOL_BUF + 1
    ext_ref[0:hb, :] = jnp.where(i > 0, halo_ref[0], 0.0)
    ext_ref[hb:hb + tm, :] = u_ref[0]
    r = i * tm + lax.broadcasted_iota(jnp.int32, (tm, 1), 0)
    for g, w in enumerate(POOL_WINDOWS):
        cols = slice(g * POOL_GW, (g + 1) * POOL_GW)
        acc = ext_ref[hb:hb + tm, cols]
        for k in range(1, w):
            acc = acc + ext_ref[hb - k:hb - k + tm, cols]
        cnt = jnp.minimum(r + 1, w).astype(F32)
        d = acc / cnt - ext_ref[hb:hb + tm, cols]
        y_ref[0, :, cols] = _dot(d.astype(BF16), pw_ref[g]) * ps_ref[:, cols]


def _pool(u, pw, ps, tm):
    bsz, seq = u.shape[:2]
    hb = POOL_BUF + 1
    ratio = tm // hb
    return pl.pallas_call(
        functools.partial(_pool_kernel, tm=tm),
        out_shape=jax.ShapeDtypeStruct(u.shape, F32),
        grid=(bsz, seq // tm),
        in_specs=[pl.BlockSpec((1, tm, D_POOL), lambda b, i: (b, i, 0)),
                  pl.BlockSpec((1, hb, D_POOL), lambda b, i: (b, jnp.maximum(i * ratio - 1, 0), 0)),
                  pl.BlockSpec(pw.shape, lambda b, i: (0, 0, 0)),
                  pl.BlockSpec(ps.shape, lambda b, i: (0, 0))],
        out_specs=pl.BlockSpec((1, tm, D_POOL), lambda b, i: (b, i, 0)),
        scratch_shapes=[pltpu.VMEM((hb + tm, D_POOL), F32)],
        compiler_params=pltpu.CompilerParams(dimension_semantics=("arbitrary", "arbitrary")),
        name="pool",
    )(u, u, pw, ps)


def _mixer_out_kernel(h_ref, py_ref, gp_ref, at_ref, ga_ref, p_ref, wo_ref, wg_ref, wp_ref, o_ref):
    mp = (py_ref[...] * _silu(gp_ref[...])).astype(BF16)
    ma = (at_ref[...] * _silu(ga_ref[...])).astype(BF16)
    h1 = h_ref[...] + _dot(mp, wo_ref[0:D_POOL, :]) + _dot(ma, wo_ref[D_POOL:D_POOL + D_ATTN, :])
    gate = jax.nn.sigmoid(_dot(h1.astype(BF16), wg_ref[...]))
    o_ref[...] = h1 + gate * _dot(p_ref[...].astype(BF16), wp_ref[...])


def _mixer_out(h, py, gp, at, ga, p, wo, wg, wp, tm):
    n = h.shape[0]
    row = lambda a: pl.BlockSpec((tm, a.shape[1]), lambda i: (i, 0))
    full = lambda a: pl.BlockSpec(a.shape, lambda i: (0, 0))
    return pl.pallas_call(
        _mixer_out_kernel,
        out_shape=jax.ShapeDtypeStruct(h.shape, F32),
        grid=(n // tm,),
        in_specs=[row(h), row(py), row(gp), row(at), row(ga), row(p), full(wo), full(wg), full(wp)],
        out_specs=pl.BlockSpec((tm, D_MODEL), lambda i: (i, 0)),
        compiler_params=pltpu.CompilerParams(dimension_semantics=("arbitrary",),
                                             vmem_limit_bytes=VMEM_LIMIT),
        name="mixer_out",
    )(h, py, gp, at, ga, p, wo, wg, wp)


def _rope_tables(pos):
    half = ROPE_DIM // 2
    inv = ROPE_THETA ** (-jnp.arange(0, ROPE_DIM, 2, dtype=F32) / ROPE_DIM)
    ang = pos.astype(F32)[:, None] * inv[None, :]
    cos, sin = jnp.cos(ang), jnp.sin(ang)
    n = pos.shape[0]
    rest = HEAD_DIM - ROPE_DIM
    c = jnp.concatenate([cos, cos, jnp.ones((n, rest), F32)], axis=1)
    s1 = jnp.concatenate([jnp.zeros((n, half), F32), sin, jnp.zeros((n, rest), F32)], axis=1)
    s2 = jnp.concatenate([-sin, jnp.zeros((n, half + rest), F32)], axis=1)
    return tuple(jnp.tile(t, (1, LANES // HEAD_DIM)) for t in (c, s1, s2))


def _cmp_to_sel(nc, n_cmp, nl):
    ci = jnp.arange(nc)[:, None] * CMP_STRIDE
    sj = jnp.arange(nl)[None, :] * SEL_LEN
    m = (ci < sj + SEL_LEN) & (ci + CMP_LEN > sj) & (jnp.arange(nc)[:, None] < n_cmp)
    return m.astype(BF16)


def _layer_params(l, norm_g, w_in, q_norm, k_norm, cmp_pe, cmp_w1, cmp_w2, pool_w, pool_scale, w_out,
                  ple_gate, ple_proj):
    w = w_in[l]
    pad = jnp.zeros((D_MODEL, D_INR - 2840), w.dtype)
    w_r = jnp.concatenate([w[:, 0:2304], w[:, 2328:2840], w[:, 2304:2328], pad], axis=1).astype(BF16)
    ones = jnp.ones((KV_W,), F32)
    gain = jnp.concatenate([jnp.tile(q_norm[l], N_HEADS) * Q_SCALE,
                            jnp.tile(k_norm[l, 0], N_KV), jnp.tile(k_norm[l, 1], N_KV),
                            jnp.tile(k_norm[l, 2], N_KV), ones, ones, ones])[None, :].astype(F32)
    eye = jnp.eye(N_KV, dtype=F32)

    def taps(w1):
        w1r = w1.reshape(2, CMP_STRIDE, HEAD_DIM, HEAD_DIM)
        return jnp.einsum('gh,atde->tgdahe', eye, w1r).reshape(CMP_STRIDE, KV_W, 2 * KV_W).astype(BF16)

    w2 = jnp.stack([jnp.kron(eye, cmp_w2[l, i]) for i in range(2)]).astype(BF16)
    pe2 = jnp.concatenate([jnp.tile(cmp_pe[l, 0], (1, N_KV)), jnp.tile(cmp_pe[l, 1], (1, N_KV))], axis=1)
    return dict(ng=norm_g[l][None, :], w=w_r, gain=gain, wk=taps(cmp_w1[l, 0]), wv=taps(cmp_w1[l, 1]),
                w2=w2, pe2=pe2, pw=pool_w[l].astype(BF16), ps=pool_scale[l][None, :],
                wo=w_out[l].astype(BF16), wg=ple_gate[l].astype(BF16), wp=ple_proj[l].astype(BF16))


def kernel(x_prompt, x_sample, cache_kv, state_win, state_pool, page_table, p_prompt, p_sample,
           norm_g, w_in, q_norm, k_norm, cmp_pe, cmp_w1, cmp_w2, pool_w, pool_scale, w_out,
           ple_gate, ple_proj):
    depth = w_in.shape[0]
    bsz, seq, _ = x_prompt.shape
    dec_b, n_pages = page_table.shape
    n_pool, page_size = cache_kv.shape[1:3]
    past = n_pages * page_size
    n_win = state_win.shape[2]
    assert page_size == PAGE and seq % 512 == 0 and seq >= WINDOW + 128 and past >= WINDOW
    assert x_sample.shape[1] == 1 and past % SEL_LEN == 0 and n_win % LANES == 0
    n_sel_p = seq // SEL_LEN
    n_sel_s = past // SEL_LEN + 1
    assert N_SELECT <= n_sel_p <= LANES
    nl_s = -(-n_sel_s // LANES) * LANES
    nc_p, nc_s = seq // CMP_STRIDE, past // CMP_STRIDE

    tm = 256
    tabs_p = _rope_tables(jnp.arange(seq))
    tabs_s = _rope_tables(jnp.full((dec_b,), past, jnp.int32))
    jm = jnp.kron(jnp.eye(LANES // HEAD_DIM, dtype=F32), jnp.full((HEAD_DIM, HEAD_DIM), 1.0 / HEAD_DIM, F32)).astype(BF16)
    c2st_p = _cmp_to_sel(nc_p, nc_p - 1, LANES).T
    c2s_s = _cmp_to_sel(nc_s, nc_s - 1, nl_s)
    ids_p = jnp.arange(bsz * seq // PAGE, dtype=jnp.int32).reshape(bsz, seq // PAGE)
    cache_t = jnp.transpose(cache_kv, (0, 1, 3, 4, 5, 2)).reshape(depth * n_pool, 4 * KV_W, PAGE)
    win_t = jnp.transpose(state_win, (0, 1, 3, 4, 5, 2)).reshape(depth, dec_b, 2 * KV_W, n_win)

    hp = x_prompt.reshape(bsz * seq, D_MODEL)
    hs = x_sample.reshape(dec_b, D_MODEL)
    outs = [[] for _ in range(6)]
    for l in range(depth):
        prm = _layer_params(l, norm_g, w_in, q_norm, k_norm, cmp_pe, cmp_w1, cmp_w2, pool_w, pool_scale,
                            w_out, ple_gate, ple_proj)
        u, gp, qpad, kv, win, ga, gbr, kaug0, kaug1, vsb, kwb, vwb = _mixer_in(
            hp, tabs_p, prm['ng'], prm['w'], prm['gain'], jm, tm)
        ck, _, cvt = _compress(ids_p, kv.reshape(bsz * seq // PAGE, PAGE, 512), prm['wk'], prm['wv'], prm['pe2'],
                               prm['w2'], False)
        r3 = lambda a: a.reshape(bsz, seq, a.shape[-1])
        attn = _nsa_prompt(r3(qpad), r3(gbr), ck, cvt, r3(kaug0), r3(kaug1), r3(vsb), r3(kwb), r3(vwb), c2st_p, 256, 512)
        py = _pool(r3(u), prm['pw'], prm['ps'], tm)
        hp = _mixer_out(hp, py.reshape(bsz * seq, D_POOL), gp, attn.reshape(bsz * seq, D_ATTN), ga,
                        p_prompt[l].reshape(bsz * seq, D_PLE), prm['wo'], prm['wg'], prm['wp'], tm)
        outs[0].append(kv.reshape(bsz, seq, 4, N_KV, HEAD_DIM))
        outs[2].append(win.reshape(bsz, seq, 2, N_KV, HEAD_DIM)[:, seq - min(WINDOW, seq):])
        outs[4].append(r3(u)[:, seq - POOL_BUF:])

        u, gp, qpad, kv, win, ga, gbr, _, _, _, _, _ = _mixer_in(
            hs, tabs_s, prm['ng'], prm['w'], prm['gain'], jm, dec_b)
        ids_s = page_table + l * n_pool
        ck, cv, _ = _compress(ids_s, cache_t, prm['wk'], prm['wv'], prm['pe2'], prm['w2'], True)
        q16 = jnp.pad(qpad.reshape(dec_b, N_HEADS, LANES), ((0, 0), (0, 16 - N_HEADS), (0, 0)))
        ocmp, idx = _nsa_sample_cmp(q16, ck, cv, c2s_s, past)
        gate16 = jnp.pad(gbr[:, :3 * N_HEADS].reshape(dec_b, N_HEADS, 3), ((0, 0), (0, 16 - N_HEADS), (0, LANES - 3)))
        new_rows = jnp.broadcast_to(jnp.concatenate([kv[:, 256:512], win], axis=1)[:, None, :], (dec_b, 8, 512))
        o16 = _nsa_sample_attn(ids_s, idx[:, :N_KV, :N_SELECT].reshape(dec_b, N_KV * N_SELECT), cache_t, q16, ocmp,
                               gate16, win_t, new_rows, l, past)
        attn = jnp.concatenate([o16[:, h, (h // Q_PER_KV) * HEAD_DIM:(h // Q_PER_KV + 1) * HEAD_DIM]
                                for h in range(N_HEADS)], axis=1)
        u_ext = jnp.concatenate([state_pool[l], u[:, None, :]], axis=1)
        py = _pool(u_ext, prm['pw'], prm['ps'], POOL_BUF + 1)[:, POOL_BUF]
        hs = _mixer_out(hs, py, gp, attn, ga, p_sample[l].reshape(dec_b, D_PLE), prm['wo'], prm['wg'], prm['wp'], dec_b)
        outs[1].append(kv.reshape(dec_b, 1, 4, N_KV, HEAD_DIM))
        outs[3].append(jnp.concatenate([state_win[l][:, 1:], win.reshape(dec_b, 1, 2, N_KV, HEAD_DIM)], axis=1))
        outs[5].append(u_ext[:, 1:])

    return (hp.reshape(bsz, seq, D_MODEL), hs.reshape(dec_b, 1, D_MODEL),
            jnp.stack(outs[0]), jnp.stack(outs[1]), jnp.stack(outs[2]), jnp.stack(outs[3]),
            jnp.stack(outs[4]), jnp.stack(outs[5]))
```

```python
import functools
import math

import jax
import jax.numpy as jnp
from jax import lax
from jax.experimental import pallas as pl
from jax.experimental.pallas import tpu as pltpu

F32 = jnp.float32
BF16 = jnp.bfloat16

D_MODEL = 1024
D_POOL = 512
POOL_WINDOWS = (2, 4, 8, 16)
POOL_GW = 128
POOL_BUF = 15
D_ATTN = 512
HEAD_DIM = 64
N_HEADS = 8
N_KV = 2
Q_PER_KV = 4
KV_W = 128
ROPE_DIM = 16
ROPE_THETA = 500000.0
CMP_LEN = 32
CMP_STRIDE = 16
SEL_LEN = 64
N_SELECT = 16
WINDOW = 512
D_PLE = 256
EPS = 1e-6
BIG = 1e30
NEG = -3.0e38
LANES = 128
D_INR = 2944
PAGE = 128
VMEM_LIMIT = 56 * 1024 * 1024
Q_SCALE = HEAD_DIM ** -0.5 * math.log2(math.e)
SHIFT_SLACK = 60.0


def _dot(a, b):
    return jnp.dot(a, b, preferred_element_type=F32)


def _dot_nt(a, b):
    return lax.dot_general(a, b, (((1,), (1,)), ((), ())), preferred_element_type=F32)


def _split_bf16(x):
    hi = x.astype(BF16)
    return hi, (x - hi.astype(F32)).astype(BF16)


def _silu(x):
    return x * jax.nn.sigmoid(x)


def _mixer_in_kernel(x_ref, cos_ref, s1_ref, s2_ref, ng_ref, w_ref, gain_ref, j_ref,
                     u_ref, gp_ref, q_ref, kv_ref, win_ref, ga_ref, gbr_ref,
                     kaug0_ref, kaug1_ref, vsb_ref, kwb_ref, vwb_ref, *, tm, nper):
    x = x_ref[...]
    ms = jnp.mean(x * x, axis=-1, keepdims=True)
    xn = (x * lax.rsqrt(ms + EPS) * ng_ref[...]).astype(BF16)
    cos, s1, s2 = cos_ref[...], s1_ref[...], s2_ref[...]
    lane = lax.broadcasted_iota(jnp.int32, (tm, LANES), 1)
    lo_half = lane < HEAD_DIM

    def mean_sq(chunks):
        wd = LANES * len(chunks)
        sq = jnp.concatenate([c * c for c in chunks], axis=1).astype(BF16)
        msq = _dot(sq, j_ref[0:wd, 0:wd])
        return [msq[:, i * LANES:(i + 1) * LANES] for i in range(len(chunks))]

    def normrope(zc, msq, gain):
        y = zc * lax.rsqrt(msq + EPS) * gain
        return y * cos + pltpu.roll(y, 8, 1) * s1 + pltpu.roll(y, LANES - 8, 1) * s2

    u_ref[...] = _dot(xn, w_ref[:, 0:512])
    gp_ref[...] = _dot(xn, w_ref[:, 512:1024])
    zq = _dot(xn, w_ref[:, 1024:1536])
    zkv = _dot(xn, w_ref[:, 1536:2304])
    zqc = [zq[:, c * LANES:(c + 1) * LANES] for c in range(4)]
    mq = mean_sq(zqc[0:2]) + mean_sq(zqc[2:4])
    mkc, mks = mean_sq([zkv[:, 0:128], zkv[:, 256:384]])
    mkw, = mean_sq([zkv[:, 512:640]])
    for c in range(4):
        yc = normrope(zqc[c], mq[c], gain_ref[:, c * LANES:(c + 1) * LANES])
        yr = pltpu.roll(yc, HEAD_DIM, 1)
        if c // 2 == 0:
            h0 = jnp.where(lo_half, yc, 0.0)
            h1 = jnp.where(lo_half, yr, 0.0)
        else:
            h0 = jnp.where(lo_half, 0.0, yr)
            h1 = jnp.where(lo_half, 0.0, yc)
        q_ref[:, (2 * c) * LANES:(2 * c + 1) * LANES] = h0.astype(BF16)
        q_ref[:, (2 * c + 1) * LANES:(2 * c + 2) * LANES] = h1.astype(BF16)
    kc = normrope(zkv[:, 0:128], mkc, gain_ref[:, 512:640])
    vc = zkv[:, 128:256]
    ks = normrope(zkv[:, 256:384], mks, gain_ref[:, 640:768])
    vs = zkv[:, 384:512]
    kw = normrope(zkv[:, 512:640], mkw, gain_ref[:, 768:896])
    vw = zkv[:, 640:768]
    kv_ref[:, 0:128] = kc
    kv_ref[:, 128:256] = vc
    kv_ref[:, 256:384] = ks
    kv_ref[:, 384:512] = vs
    win_ref[:, 0:128] = kw
    win_ref[:, 128:256] = vw
    ga_ref[...] = _dot(xn, w_ref[:, 2304:2816])
    gbr_ref[...] = jax.nn.sigmoid(_dot(xn, w_ref[:, 2816:2944]))
    pos = (pl.program_id(0) % nper) * tm + lax.broadcasted_iota(jnp.int32, (tm, LANES), 0)
    ones = jnp.ones((tm, LANES), BF16)
    onehot = jnp.where(lane == pos // SEL_LEN, 1.0, 0.0).astype(BF16)
    kaug0_ref[:, 0:128] = jnp.where(lo_half, ks, jnp.where(lane == HEAD_DIM, 1.0, 0.0)).astype(BF16)
    kaug0_ref[:, 128:256] = onehot
    kaug1_ref[:, 0:128] = jnp.where(lo_half, jnp.where(lane == 0, 1.0, 0.0), ks).astype(BF16)
    kaug1_ref[:, 128:256] = onehot
    vsb_ref[:, 0:128] = vs.astype(BF16)
    vsb_ref[:, 128:256] = ones
    kwb_ref[...] = kw.astype(BF16)
    vwb_ref[:, 0:128] = vw.astype(BF16)
    vwb_ref[:, 128:256] = ones


def _mixer_in(x, tabs, ng, w, gain, jm, tm):
    n = x.shape[0]
    nper = tabs[0].shape[0] // tm
    row = lambda width: pl.BlockSpec((tm, width), lambda i: (i, 0))
    full = lambda a: pl.BlockSpec(a.shape, lambda i: (0,) * a.ndim)
    tab = pl.BlockSpec((tm, LANES), lambda i: (i % nper, 0))
    widths = [(512, F32), (512, F32), (1024, BF16), (512, F32), (256, F32), (512, F32), (128, F32),
              (256, BF16), (256, BF16), (256, BF16), (128, BF16), (256, BF16)]
    return pl.pallas_call(
        functools.partial(_mixer_in_kernel, tm=tm, nper=nper),
        out_shape=[jax.ShapeDtypeStruct((n, wd), dt) for wd, dt in widths],
        grid=(n // tm,),
        in_specs=[row(D_MODEL), tab, tab, tab, full(ng), full(w), full(gain), full(jm)],
        out_specs=[row(wd) for wd, _ in widths],
        compiler_params=pltpu.CompilerParams(dimension_semantics=("arbitrary",),
                                             vmem_limit_bytes=VMEM_LIMIT),
        name="mixer_in",
    )(x, *tabs, ng, w, gain, jm)


def _compress_kernel(pt_ref, src_ref, wk_ref, wv_ref, pe_ref, w2_ref, ck_ref, cv_ref, cvt_ref,
                     *scratch, n_pages, nb, feature_major):
    if feature_major:
        stage_ref, buf_ref, bias_ref, sem_ref = scratch
    else:
        buf_ref, bias_ref, sem_ref = scratch
    b = pl.program_id(0)
    nchunk = n_pages * PAGE // CMP_STRIDE
    slot = b % 2

    def page_copies(bb, sl, j):
        if feature_major:
            return [pltpu.make_async_copy(src_ref.at[pt_ref[bb, j], pl.ds(0, 2 * KV_W), :],
                                          stage_ref.at[sl, j], sem_ref.at[sl])]
        return [pltpu.make_async_copy(src_ref.at[pt_ref[bb, j], :, pl.ds(i * KV_W, KV_W)],
                                      buf_ref.at[sl, i, pl.ds(j * PAGE, PAGE), :], sem_ref.at[sl])
                for i in range(2)]

    def start_all(bb, sl):
        def body(j, c):
            for cp in page_copies(bb, sl, j):
                cp.start()
            return c
        lax.fori_loop(0, n_pages, body, 0)

    def wait_all(bb, sl):
        def body(j, c):
            for cp in page_copies(bb, sl, j):
                cp.wait()
            return c
        lax.fori_loop(0, n_pages, body, 0)

    @pl.when(b == 0)
    def _():
        start_all(0, 0)
        acc = jnp.zeros((8, 256), F32)
        for t in range(CMP_STRIDE):
            a0 = jnp.broadcast_to(pe_ref[t:t + 1, :], (8, 256)).astype(BF16)
            a1 = jnp.broadcast_to(pe_ref[CMP_STRIDE + t:CMP_STRIDE + t + 1, :], (8, 256)).astype(BF16)
            pk0 = _dot(a0[:, 0:128], wk_ref[t])
            pk1 = _dot(a1[:, 0:128], wk_ref[t])
            pv0 = _dot(a0[:, 128:256], wv_ref[t])
            pv1 = _dot(a1[:, 128:256], wv_ref[t])
            acc = acc + jnp.concatenate([pk0[:, 0:128] + pk1[:, 128:256],
                                         pv0[:, 0:128] + pv1[:, 128:256]], axis=1)
        bias_ref[...] = acc

    @pl.when(b + 1 < nb)
    def _():
        start_all(b + 1, 1 - slot)

    wait_all(b, slot)

    if feature_major:
        def tr(j, c):
            xt = stage_ref[slot, j]
            r0 = pl.multiple_of(j * PAGE, PAGE)
            buf_ref[0, pl.ds(r0, PAGE), :] = xt[0:KV_W, :].T
            buf_ref[1, pl.ds(r0, PAGE), :] = xt[KV_W:2 * KV_W, :].T
            return c
        lax.fori_loop(0, n_pages, tr, 0, unroll=8)
        rows = lambda i: buf_ref.at[i]
    else:
        rows = lambda i: buf_ref.at[slot, i]

    acc_k = jnp.zeros((nchunk, 256), F32)
    acc_v = jnp.zeros((nchunk, 256), F32)
    for t in range(CMP_STRIDE):
        xk = rows(0)[pl.ds(t, nchunk, stride=CMP_STRIDE), :]
        xv = rows(1)[pl.ds(t, nchunk, stride=CMP_STRIDE), :]
        acc_k = acc_k + _dot(xk.astype(BF16), wk_ref[t])
        acc_v = acc_v + _dot(xv.astype(BF16), wv_ref[t])
    row = lax.broadcasted_iota(jnp.int32, (nchunk, LANES), 0)
    valid = row < nchunk - 1
    res = []
    for i, acc in enumerate((acc_k, acc_v)):
        pre = acc[:, 0:128] + pltpu.roll(acc[:, 128:256], nchunk - 1, 0) + bias_ref[0:1, i * 128:(i + 1) * 128]
        hid = _silu(pre).astype(BF16)
        res.append(jnp.where(valid, _dot(hid, w2_ref[i]), 0.0))
    ck_ref[0] = res[0].astype(BF16)
    cv_ref[0] = res[1].astype(BF16)
    cvt_ref[0] = res[1].T.astype(BF16)


def _compress(page_ids, src, wk, wv, pe2, w2, feature_major):
    nb, n_pages = page_ids.shape
    nchunk = n_pages * PAGE // CMP_STRIDE
    full = lambda a: pl.BlockSpec(a.shape, lambda b, pt: (0,) * a.ndim)
    if feature_major:
        scratch = [pltpu.VMEM((2, n_pages, 2 * KV_W, PAGE), F32), pltpu.VMEM((2, n_pages * PAGE, KV_W), F32)]
    else:
        scratch = [pltpu.VMEM((2, 2, n_pages * PAGE, KV_W), F32)]
    return pl.pallas_call(
        functools.partial(_compress_kernel, n_pages=n_pages, nb=nb, feature_major=feature_major),
        out_shape=[jax.ShapeDtypeStruct((nb, nchunk, KV_W), BF16), jax.ShapeDtypeStruct((nb, nchunk, KV_W), BF16),
                   jax.ShapeDtypeStruct((nb, KV_W, nchunk), BF16)],
        grid_spec=pltpu.PrefetchScalarGridSpec(
            num_scalar_prefetch=1,
            grid=(nb,),
            in_specs=[pl.BlockSpec(memory_space=pl.ANY), full(wk), full(wv), full(pe2), full(w2)],
            out_specs=[pl.BlockSpec((1, nchunk, KV_W), lambda b, pt: (b, 0, 0)),
                       pl.BlockSpec((1, nchunk, KV_W), lambda b, pt: (b, 0, 0)),
                       pl.BlockSpec((1, KV_W, nchunk), lambda b, pt: (b, 0, 0))],
            scratch_shapes=scratch + [pltpu.VMEM((8, 256), F32), pltpu.SemaphoreType.DMA((2,))],
        ),
        compiler_params=pltpu.CompilerParams(dimension_semantics=("arbitrary",),
                                             vmem_limit_bytes=VMEM_LIMIT),
        name="compress",
    )(page_ids, src, wk, wv, pe2, w2)


def _select_topk(x, blk, axis, on_pick):
    nl = float(x.shape[axis])
    blkf = blk.astype(F32)
    for it in range(N_SELECT):
        m = jnp.max(x, axis=axis, keepdims=True)
        idx = jnp.min(jnp.where(x == m, blkf, nl), axis=axis, keepdims=True)
        chosen = blkf == idx
        on_pick(it, idx, chosen, m > -0.5 * BIG)
        x = jnp.where(chosen, NEG, x)


def _importance(imp, blk, cur):
    d = cur - blk
    return jnp.where(blk == 0, BIG, jnp.where(d == 0, BIG, jnp.where(d == 1, BIG, jnp.where(d > 1, imp, -BIG))))


def _nsa_prompt_kernel(q_ref, gbr_ref, ck_ref, cvt_ref, kaug0_ref, kaug1_ref, vs_ref, kw_ref, vw_ref, c2st_ref,
                       o_ref, lhs_ref, acc_ref, bak_ref, c_ref, p0_ref, p1_ref, *, tq, tk):
    t0 = pl.program_id(1) * tq
    rows = Q_PER_KV * tq
    nc = ck_ref.shape[1]
    kaug_refs = (kaug0_ref, kaug1_ref)
    lane_r = lax.broadcasted_iota(jnp.int32, (rows, LANES), 1)
    qpos_r = t0 + lax.broadcasted_iota(jnp.int32, (rows, 1), 0) % tq
    qpos_c = t0 + lax.broadcasted_iota(jnp.int32, (1, rows), 1) % tq
    qpos_t = t0 + lax.broadcasted_iota(jnp.int32, (1, tq), 1)
    blk_t = lax.broadcasted_iota(jnp.int32, (LANES, tq), 0)
    cend = lax.broadcasted_iota(jnp.int32, (nc, 1), 0) * CMP_STRIDE + (CMP_LEN - 1)
    lo_half = lax.broadcasted_iota(jnp.int32, (tq, LANES), 1) < HEAD_DIM
    nwin = WINDOW + tq
    kstart = pl.multiple_of(jnp.maximum(t0 - WINDOW, 0), tq)
    cmask = cend <= qpos_c

    o_cmp, o_win = [], []
    for g in range(N_KV):
        for r in range(Q_PER_KV):
            h = Q_PER_KV * g + r
            lhs_ref[g, r * tq:(r + 1) * tq, 0:LANES] = q_ref[0, :, h * LANES:(h + 1) * LANES]
        qg = lhs_ref[g, :, 0:LANES]

        s = jnp.where(cmask, _dot_nt(ck_ref[0], qg), -BIG)
        m = jnp.max(s, axis=0, keepdims=True)
        e = jnp.exp2(s - m)
        p = e * jnp.where(m > -0.5 * BIG, 1.0 / jnp.sum(e, axis=0, keepdims=True), 0.0)
        o_cmp.append(_dot(cvt_ref[0], p.astype(BF16)).T)
        psum = p[:, 0:tq] + p[:, tq:2 * tq] + p[:, 2 * tq:3 * tq] + p[:, 3 * tq:4 * tq]
        hi, lo = _split_bf16(psum)
        imp = _dot(c2st_ref[...], hi) + _dot(c2st_ref[...], lo)

        bias = [jnp.full((LANES, tq), -BIG, F32)]

        def on_pick(it, idx, chosen, valid):
            bias[0] = jnp.where(chosen, jnp.where(valid, 0.0, -BIG), bias[0])

        _select_topk(_importance(imp, blk_t, qpos_t // SEL_LEN), blk_t, 0, on_pick)
        selbias = bias[0].T.astype(BF16)
        for r in range(Q_PER_KV):
            lhs_ref[g, r * tq:(r + 1) * tq, LANES:2 * LANES] = selbias

        kwpos = kstart + lax.broadcasted_iota(jnp.int32, (1, nwin), 1)
        wmask = (kwpos <= qpos_r) & (kwpos >= qpos_r - WINDOW)
        s = jnp.where(wmask, _dot_nt(qg, kw_ref[0, pl.ds(kstart, nwin), :]), -BIG)
        e = jnp.exp2(s - jnp.max(s, axis=1, keepdims=True))
        ow = _dot(e.astype(BF16), vw_ref[0, pl.ds(kstart, nwin), :])
        o_win.append(ow[:, 0:LANES] / ow[:, LANES:2 * LANES])

    last_tile = kaug0_ref.shape[1] // tk - 1

    def key_rows(kt):
        return pl.ds(pl.multiple_of(jnp.minimum(kt, last_tile) * tk, tk), tk)

    def scores(g, kt):
        causal = (kt * tk + lax.broadcasted_iota(jnp.int32, (1, tk), 1)) <= qpos_r
        return jnp.where(causal, _dot_nt(lhs_ref[g], kaug_refs[g][0, key_rows(kt), :]), -BIG)

    def add_pv(g, p, kt):
        acc_ref[g] = acc_ref[g] + _dot(p, vs_ref[0, key_rows(kt), :])

    def recentre(g, kt, p_out, first):
        s_ = scores(g, kt)
        if not first:
            c_old = c_ref[g][:, 0:1]
            s_ = s_ + c_old
        mr = jnp.max(s_, axis=1, keepdims=True)
        c_new = (mr if first else jnp.maximum(c_old, mr)).astype(BF16).astype(F32)
        p_out[g] = jnp.exp2(s_ - c_new).astype(BF16)
        acc_ref[g] = jnp.zeros((rows, 2 * LANES), F32) if first else acc_ref[g] * jnp.exp2(c_old - c_new)
        c_ref[g] = jnp.broadcast_to(c_new, (rows, LANES))
        qf = lhs_ref[g, :, 0:LANES].astype(F32)
        lhs_ref[g, :, 0:LANES] = jnp.where(lane_r == HEAD_DIM * (1 - g), -c_new, qf).astype(BF16)

    for g in range(N_KV):
        recentre(g, 0, p0_ref, True)

    def fast_tile(g, kt, p_in, p_out):
        add_pv(g, p_in[g], kt - 1)
        s_ = scores(g, kt)
        p_out[g] = jnp.exp2(s_).astype(BF16)
        return jnp.max(s_)

    def sel_pair(j, carry):
        a = 2 * j + 1
        for g in range(N_KV):
            bak_ref[g] = acc_ref[g]
        top = [fast_tile(g, a, p0_ref, p1_ref) for g in range(N_KV)]
        top += [fast_tile(g, a + 1, p1_ref, p0_ref) for g in range(N_KV)]

        @pl.when(jnp.logical_not(jnp.maximum(jnp.maximum(top[0], top[1]), jnp.maximum(top[2], top[3])) <= SHIFT_SLACK))
        def _():
            for g in range(N_KV):
                acc_ref[g] = bak_ref[g]
                add_pv(g, jnp.exp2(scores(g, a - 1)).astype(BF16), a - 1)
                recentre(g, a, p1_ref, False)
                add_pv(g, p1_ref[g], a)
                recentre(g, a + 1, p0_ref, False)

        return carry

    n_pairs = ((t0 + tq + tk - 1) // tk) // 2
    lax.fori_loop(0, n_pairs, sel_pair, 0)
    for g in range(N_KV):
        add_pv(g, p0_ref[g], 2 * n_pairs)


    for g in range(N_KV):
        acc = acc_ref[g]
        o_sel = acc[:, 0:LANES] / acc[:, LANES:2 * LANES]
        pieces = []
        for r in range(Q_PER_KV):
            h = Q_PER_KV * g + r
            sl = slice(r * tq, (r + 1) * tq)
            oh = (o_cmp[g][sl] * gbr_ref[0, :, 3 * h:3 * h + 1] + o_sel[sl] * gbr_ref[0, :, 3 * h + 1:3 * h + 2]
                  + o_win[g][sl] * gbr_ref[0, :, 3 * h + 2:3 * h + 3])
            pieces.append(oh if (h % 2) == g else pltpu.roll(oh, HEAD_DIM, 1))
        for c in range(2):
            o_ref[0, :, (2 * g + c) * LANES:(2 * g + c + 1) * LANES] = jnp.where(lo_half, pieces[2 * c], pieces[2 * c + 1])


def _nsa_prompt(qpad, gbr, ck, cvt, kaug0, kaug1, vsb, kwb, vwb, c2st, tq, tk):
    bsz, seq = qpad.shape[:2]
    rows = Q_PER_KV * tq
    per_b = lambda a: pl.BlockSpec((1,) + a.shape[1:], lambda b, i: (b, 0, 0), pipeline_mode=pl.Buffered(1))
    return pl.pallas_call(
        functools.partial(_nsa_prompt_kernel, tq=tq, tk=tk),
        out_shape=jax.ShapeDtypeStruct((bsz, seq, D_ATTN), F32),
        grid=(bsz, seq // tq),
        in_specs=[pl.BlockSpec((1, tq, N_HEADS * LANES), lambda b, i: (b, i, 0)),
                  pl.BlockSpec((1, tq, LANES), lambda b, i: (b, i, 0)),
                  per_b(ck), per_b(cvt), per_b(kaug0), per_b(kaug1), per_b(vsb), per_b(kwb), per_b(vwb),
                  pl.BlockSpec(c2st.shape, lambda b, i: (0, 0))],
        out_specs=pl.BlockSpec((1, tq, D_ATTN), lambda b, i: (b, i, 0)),
        scratch_shapes=[pltpu.VMEM((N_KV, rows, 2 * LANES), BF16),
                        pltpu.VMEM((N_KV, rows, 2 * LANES), F32),
                        pltpu.VMEM((N_KV, rows, 2 * LANES), F32),
                        pltpu.VMEM((N_KV, rows, LANES), F32),
                        pltpu.VMEM((N_KV, rows, tk), BF16),
                        pltpu.VMEM((N_KV, rows, tk), BF16)],
        compiler_params=pltpu.CompilerParams(dimension_semantics=("arbitrary", "arbitrary"),
                                             vmem_limit_bytes=VMEM_LIMIT),
        name="nsa_prompt",
    )(qpad, gbr, ck, cvt, kaug0, kaug1, vsb, kwb, vwb, c2st)


def _nsa_sample_cmp_kernel(q_ref, ck_ref, cv_ref, c2s_ref, ocmp_ref, idx_ref, imp_ref, *, past, nb):
    nc = ck_ref.shape[1]
    nl = c2s_ref.shape[1]
    cend = lax.broadcasted_iota(jnp.int32, (1, nc), 1) * CMP_STRIDE + (CMP_LEN - 1)
    cmask = cend <= past
    row = lax.broadcasted_iota(jnp.int32, (8, 1), 0)

    def one_sequence(b, carry):
        q = q_ref[b]
        s = jnp.where(cmask, _dot_nt(q, ck_ref[b]), -BIG)
        m = jnp.max(s, axis=1, keepdims=True)
        e = jnp.exp2(s - m)
        p = e * jnp.where(m > -0.5 * BIG, 1.0 / jnp.sum(e, axis=1, keepdims=True), 0.0)
        ocmp_ref[b] = _dot(p.astype(BF16), cv_ref[b])
        ps0 = jnp.sum(p[0:Q_PER_KV], axis=0, keepdims=True)
        ps1 = jnp.sum(p[Q_PER_KV:2 * Q_PER_KV], axis=0, keepdims=True)
        psum = jnp.where(row == 0, ps0, jnp.where(row == 1, ps1, 0.0))
        hi, lo = _split_bf16(psum)
        imp_ref[pl.ds(pl.multiple_of(b * 8, 8), 8), :] = _dot(hi, c2s_ref[...]) + _dot(lo, c2s_ref[...])
        return carry

    lax.fori_loop(0, nb, one_sequence, 0)

    blk = lax.broadcasted_iota(jnp.int32, (nb * 8, nl), 1)
    out_lane = lax.broadcasted_iota(jnp.int32, (nb * 8, LANES), 1)
    picks = [jnp.full((nb * 8, LANES), -1, jnp.int32)]

    def on_pick(it, idx, chosen, valid):
        picks[0] = jnp.where(out_lane == it, jnp.where(valid, idx.astype(jnp.int32), -1), picks[0])

    _select_topk(_importance(imp_ref[...], blk, past // SEL_LEN), blk, 1, on_pick)
    idx_ref[...] = picks[0]


def _nsa_sample_cmp(q16, ck, cv, c2s, past):
    bsz = q16.shape[0]
    full = lambda a: pl.BlockSpec(a.shape, lambda i: (0,) * a.ndim)
    return pl.pallas_call(
        functools.partial(_nsa_sample_cmp_kernel, past=past, nb=bsz),
        out_shape=[jax.ShapeDtypeStruct((bsz, 16, LANES), F32), jax.ShapeDtypeStruct((bsz * 8, LANES), jnp.int32)],
        grid=(1,),
        in_specs=[full(q16), full(ck), full(cv), full(c2s)],
        out_specs=[pl.BlockSpec((bsz, 16, LANES), lambda i: (0, 0, 0)),
                   pl.BlockSpec((bsz * 8, LANES), lambda i: (0, 0))],
        scratch_shapes=[pltpu.VMEM((bsz * 8, c2s.shape[1]), F32)],
        compiler_params=pltpu.CompilerParams(dimension_semantics=("arbitrary",),
                                             vmem_limit_bytes=VMEM_LIMIT),
        name="nsa_sample_cmp",
    )(q16, ck, cv, c2s)


def _nsa_sample_attn_kernel(pt_ref, idx_ref, cache_ref, q_ref, ocmp_ref, gate_ref, win_ref, new_ref,
                            o_ref, buf_ref, sem_ref, *, past, nb):
    b = pl.program_id(0)
    nblk = N_KV * N_SELECT
    npast_blk = past // SEL_LEN
    per_page = PAGE // SEL_LEN
    slot = b % 2

    def blk_copy(bb, sl, i):
        j = jnp.clip(idx_ref[bb, i], 0, npast_blk - 1)
        return pltpu.make_async_copy(
            cache_ref.at[pt_ref[bb, j // per_page], pl.ds(2 * KV_W, 2 * KV_W), :],
            buf_ref.at[sl, i // N_SELECT, :, pl.ds(pl.multiple_of((i % N_SELECT) * PAGE, PAGE), PAGE)],
            sem_ref.at[sl])

    def start_all(bb, sl):
        def body(i, c):
            blk_copy(bb, sl, i).start()
            return c
        lax.fori_loop(0, nblk, body, 0)

    def wait_all(bb, sl):
        def body(i, c):
            blk_copy(bb, sl, i).wait()
            return c
        lax.fori_loop(0, nblk, body, 0)

    @pl.when(b == 0)
    def _():
        start_all(0, 0)

    @pl.when(b + 1 < nb)
    def _():
        start_all(b + 1, 1 - slot)

    q = q_ref[0]
    qf = q.astype(F32)
    knew = new_ref[0, 0:1, :]
    ks_new = knew[:, 0:128].astype(BF16).astype(F32)
    vs_new = knew[:, 128:256].astype(BF16).astype(F32)
    kw_new = knew[:, 256:384].astype(BF16).astype(F32)
    vw_new = knew[:, 384:512].astype(BF16).astype(F32)
    row = lax.broadcasted_iota(jnp.int32, (16, 1), 0)

    s = _dot(q, win_ref[0, 0:KV_W, :].astype(BF16))
    s_new = jnp.sum(qf * kw_new, axis=1, keepdims=True)
    m = jnp.maximum(jnp.max(s, axis=1, keepdims=True), s_new)
    e = jnp.exp2(s - m)
    e_new = jnp.exp2(s_new - m)
    o_win = ((_dot_nt(e.astype(BF16), win_ref[0, KV_W:2 * KV_W, :].astype(BF16))
              + e_new.astype(BF16).astype(F32) * vw_new) / (jnp.sum(e, axis=1, keepdims=True) + e_new))

    wait_all(b, slot)

    nkeys = N_SELECT * PAGE
    lane = lax.broadcasted_iota(jnp.int32, (1, nkeys), 1)
    s_new = jnp.sum(qf * ks_new, axis=1, keepdims=True)
    o_sel = jnp.zeros((16, LANES), F32)
    for g in range(N_KV):
        kpos = jnp.full((1, nkeys), past + 1, jnp.int32)
        for n in range(N_SELECT):
            j = idx_ref[b, g * N_SELECT + n]
            ok = (j >= 0) & (j < npast_blk)
            in_blk = (lane // PAGE == n) & ((lane % PAGE) // SEL_LEN == j % per_page) & ok
            kpos = jnp.where(in_blk, (j // per_page) * PAGE + lane % PAGE, kpos)
        s = jnp.where(kpos <= past, _dot(q, buf_ref[slot, g, 0:KV_W, :].astype(BF16)), -BIG)
        m = jnp.maximum(jnp.max(s, axis=1, keepdims=True), s_new)
        e = jnp.exp2(s - m)
        e_new = jnp.exp2(s_new - m)
        og = ((_dot_nt(e.astype(BF16), buf_ref[slot, g, KV_W:2 * KV_W, :].astype(BF16))
               + e_new.astype(BF16).astype(F32) * vs_new) / (jnp.sum(e, axis=1, keepdims=True) + e_new))
        o_sel = jnp.where((row >= g * Q_PER_KV) & (row < (g + 1) * Q_PER_KV), og, o_sel)

    gt = gate_ref[0]
    o_ref[0] = ocmp_ref[0] * gt[:, 0:1] + o_sel * gt[:, 1:2] + o_win * gt[:, 2:3]


def _nsa_sample_attn(page_ids, idx, cache_t, q16, ocmp, gate16, win_t, new_rows, layer, past):
    bsz = q16.shape[0]
    nwin = win_t.shape[3]
    return pl.pallas_call(
        functools.partial(_nsa_sample_attn_kernel, past=past, nb=bsz),
        out_shape=jax.ShapeDtypeStruct((bsz, 16, LANES), F32),
        grid_spec=pltpu.PrefetchScalarGridSpec(
            num_scalar_prefetch=2,
            grid=(bsz,),
            in_specs=[pl.BlockSpec(memory_space=pl.ANY),
                      pl.BlockSpec((1, 16, LANES), lambda b, pt, ix: (b, 0, 0)),
                      pl.BlockSpec((1, 16, LANES), lambda b, pt, ix: (b, 0, 0)),
                      pl.BlockSpec((1, 16, LANES), lambda b, pt, ix: (b, 0, 0)),
                      pl.BlockSpec((None, 1, 2 * KV_W, nwin), lambda b, pt, ix: (layer, b, 0, 0)),
                      pl.BlockSpec((1, 8, 512), lambda b, pt, ix: (b, 0, 0))],
            out_specs=pl.BlockSpec((1, 16, LANES), lambda b, pt, ix: (b, 0, 0)),
            scratch_shapes=[pltpu.VMEM((2, N_KV, 2 * KV_W, N_SELECT * PAGE), F32),
                            pltpu.SemaphoreType.DMA((2,))],
        ),
        compiler_params=pltpu.CompilerParams(dimension_semantics=("arbitrary",)),
        name="nsa_sample_attn",
    )(page_ids, idx, cache_t, q16, ocmp, gate16, win_t, new_rows)


def _pool_kernel(u_ref, halo_ref, pw_ref, ps_ref, y_ref, ext_ref, *, tm):
    i = pl.program_id(1)
    hb = POOL_BUF + 1
    ext_ref[0:hb, :] = jnp.where(i > 0, halo_ref[0], 0.0)
    ext_ref[hb:hb + tm, :] = u_ref[0]
    r = i * tm + lax.broadcasted_iota(jnp.int32, (tm, 1), 0)
    for g, w in enumerate(POOL_WINDOWS):
        cols = slice(g * POOL_GW, (g + 1) * POOL_GW)
        acc = ext_ref[hb:hb + tm, cols]
        for k in range(1, w):
            acc = acc + ext_ref[hb - k:hb - k + tm, cols]
        cnt = jnp.minimum(r + 1, w).astype(F32)
        d = acc / cnt - ext_ref[hb:hb + tm, cols]
        y_ref[0, :, cols] = _dot(d.astype(BF16), pw_ref[g]) * ps_ref[:, cols]


def _pool(u, pw, ps, tm):
    bsz, seq = u.shape[:2]
    hb = POOL_BUF + 1
    ratio = tm // hb
    return pl.pallas_call(
        functools.partial(_pool_kernel, tm=tm),
        out_shape=jax.ShapeDtypeStruct(u.shape, F32),
        grid=(bsz, seq // tm),
        in_specs=[pl.BlockSpec((1, tm, D_POOL), lambda b, i: (b, i, 0)),
                  pl.BlockSpec((1, hb, D_POOL), lambda b, i: (b, jnp.maximum(i * ratio - 1, 0), 0)),
                  pl.BlockSpec(pw.shape, lambda b, i: (0, 0, 0)),
                  pl.BlockSpec(ps.shape, lambda b, i: (0, 0))],
        out_specs=pl.BlockSpec((1, tm, D_POOL), lambda b, i: (b, i, 0)),
        scratch_shapes=[pltpu.VMEM((hb + tm, D_POOL), F32)],
        compiler_params=pltpu.CompilerParams(dimension_semantics=("arbitrary", "arbitrary")),
        name="pool",
    )(u, u, pw, ps)


def _mixer_out_kernel(h_ref, py_ref, gp_ref, at_ref, ga_ref, p_ref, wo_ref, wg_ref, wp_ref, o_ref):
    mp = (py_ref[...] * _silu(gp_ref[...])).astype(BF16)
    ma = (at_ref[...] * _silu(ga_ref[...])).astype(BF16)
    h1 = h_ref[...] + _dot(mp, wo_ref[0:D_POOL, :]) + _dot(ma, wo_ref[D_POOL:D_POOL + D_ATTN, :])
    gate = jax.nn.sigmoid(_dot(h1.astype(BF16), wg_ref[...]))
    o_ref[...] = h1 + gate * _dot(p_ref[...].astype(BF16), wp_ref[...])


def _mixer_out(h, py, gp, at, ga, p, wo, wg, wp, tm):
    n = h.shape[0]
    row = lambda a: pl.BlockSpec((tm, a.shape[1]), lambda i: (i, 0))
    full = lambda a: pl.BlockSpec(a.shape, lambda i: (0, 0))
    return pl.pallas_call(
        _mixer_out_kernel,
        out_shape=jax.ShapeDtypeStruct(h.shape, F32),
        grid=(n // tm,),
        in_specs=[row(h), row(py), row(gp), row(at), row(ga), row(p), full(wo), full(wg), full(wp)],
        out_specs=pl.BlockSpec((tm, D_MODEL), lambda i: (i, 0)),
        compiler_params=pltpu.CompilerParams(dimension_semantics=("arbitrary",),
                                             vmem_limit_bytes=VMEM_LIMIT),
        name="mixer_out",
    )(h, py, gp, at, ga, p, wo, wg, wp)


def _rope_tables(pos):
    half = ROPE_DIM // 2
    inv = ROPE_THETA ** (-jnp.arange(0, ROPE_DIM, 2, dtype=F32) / ROPE_DIM)
    ang = pos.astype(F32)[:, None] * inv[None, :]
    cos, sin = jnp.cos(ang), jnp.sin(ang)
    n = pos.shape[0]
    rest = HEAD_DIM - ROPE_DIM
    c = jnp.concatenate([cos, cos, jnp.ones((n, rest), F32)], axis=1)
    s1 = jnp.concatenate([jnp.zeros((n, half), F32), sin, jnp.zeros((n, rest), F32)], axis=1)
    s2 = jnp.concatenate([-sin, jnp.zeros((n, half + rest), F32)], axis=1)
    return tuple(jnp.tile(t, (1, LANES // HEAD_DIM)) for t in (c, s1, s2))


def _cmp_to_sel(nc, n_cmp, nl):
    ci = jnp.arange(nc)[:, None] * CMP_STRIDE
    sj = jnp.arange(nl)[None, :] * SEL_LEN
    m = (ci < sj + SEL_LEN) & (ci + CMP_LEN > sj) & (jnp.arange(nc)[:, None] < n_cmp)
    return m.astype(BF16)


def _layer_params(l, norm_g, w_in, q_norm, k_norm, cmp_pe, cmp_w1, cmp_w2, pool_w, pool_scale, w_out,
                  ple_gate, ple_proj):
    w = w_in[l]
    pad = jnp.zeros((D_MODEL, D_INR - 2840), w.dtype)
    w_r = jnp.concatenate([w[:, 0:2304], w[:, 2328:2840], w[:, 2304:2328], pad], axis=1).astype(BF16)
    ones = jnp.ones((KV_W,), F32)
    gain = jnp.concatenate([jnp.tile(q_norm[l], N_HEADS) * Q_SCALE,
                            jnp.tile(k_norm[l, 0], N_KV), jnp.tile(k_norm[l, 1], N_KV),
                            jnp.tile(k_norm[l, 2], N_KV), ones, ones, ones])[None, :].astype(F32)
    eye = jnp.eye(N_KV, dtype=F32)

    def taps(w1):
        w1r = w1.reshape(2, CMP_STRIDE, HEAD_DIM, HEAD_DIM)
        return jnp.einsum('gh,atde->tgdahe', eye, w1r).reshape(CMP_STRIDE, KV_W, 2 * KV_W).astype(BF16)

    w2 = jnp.stack([jnp.kron(eye, cmp_w2[l, i]) for i in range(2)]).astype(BF16)
    pe2 = jnp.concatenate([jnp.tile(cmp_pe[l, 0], (1, N_KV)), jnp.tile(cmp_pe[l, 1], (1, N_KV))], axis=1)
    return dict(ng=norm_g[l][None, :], w=w_r, gain=gain, wk=taps(cmp_w1[l, 0]), wv=taps(cmp_w1[l, 1]),
                w2=w2, pe2=pe2, pw=pool_w[l].astype(BF16), ps=pool_scale[l][None, :],
                wo=w_out[l].astype(BF16), wg=ple_gate[l].astype(BF16), wp=ple_proj[l].astype(BF16))


def kernel(x_prompt, x_sample, cache_kv, state_win, state_pool, page_table, p_prompt, p_sample,
           norm_g, w_in, q_norm, k_norm, cmp_pe, cmp_w1, cmp_w2, pool_w, pool_scale, w_out,
           ple_gate, ple_proj):
    depth = w_in.shape[0]
    bsz, seq, _ = x_prompt.shape
    dec_b, n_pages = page_table.shape
    n_pool, page_size = cache_kv.shape[1:3]
    past = n_pages * page_size
    n_win = state_win.shape[2]
    assert page_size == PAGE and seq % 512 == 0 and seq >= WINDOW + 128 and past >= WINDOW
    assert x_sample.shape[1] == 1 and past % SEL_LEN == 0 and n_win % LANES == 0
    n_sel_p = seq // SEL_LEN
    n_sel_s = past // SEL_LEN + 1
    assert N_SELECT <= n_sel_p <= LANES
    nl_s = -(-n_sel_s // LANES) * LANES
    nc_p, nc_s = seq // CMP_STRIDE, past // CMP_STRIDE

    tm = 256
    tabs_p = _rope_tables(jnp.arange(seq))
    tabs_s = _rope_tables(jnp.full((dec_b,), past, jnp.int32))
    jm = jnp.kron(jnp.eye(2 * LANES // HEAD_DIM, dtype=F32), jnp.full((HEAD_DIM, HEAD_DIM), 1.0 / HEAD_DIM, F32)).astype(BF16)
    c2st_p = _cmp_to_sel(nc_p, nc_p - 1, LANES).T
    c2s_s = _cmp_to_sel(nc_s, nc_s - 1, nl_s)
    ids_p = jnp.arange(bsz * seq // PAGE, dtype=jnp.int32).reshape(bsz, seq // PAGE)
    cache_t = jnp.transpose(cache_kv, (0, 1, 3, 4, 5, 2)).reshape(depth * n_pool, 4 * KV_W, PAGE)
    win_t = jnp.transpose(state_win, (0, 1, 3, 4, 5, 2)).reshape(depth, dec_b, 2 * KV_W, n_win)

    hp = x_prompt.reshape(bsz * seq, D_MODEL)
    hs = x_sample.reshape(dec_b, D_MODEL)
    outs = [[] for _ in range(6)]
    for l in range(depth):
        prm = _layer_params(l, norm_g, w_in, q_norm, k_norm, cmp_pe, cmp_w1, cmp_w2, pool_w, pool_scale,
                            w_out, ple_gate, ple_proj)
        u, gp, qpad, kv, win, ga, gbr, kaug0, kaug1, vsb, kwb, vwb = _mixer_in(
            hp, tabs_p, prm['ng'], prm['w'], prm['gain'], jm, tm)
        ck, _, cvt = _compress(ids_p, kv.reshape(bsz * seq // PAGE, PAGE, 512), prm['wk'], prm['wv'], prm['pe2'],
                               prm['w2'], False)
        r3 = lambda a: a.reshape(bsz, seq, a.shape[-1])
        attn = _nsa_prompt(r3(qpad), r3(gbr), ck, cvt, r3(kaug0), r3(kaug1), r3(vsb), r3(kwb), r3(vwb), c2st_p, 256, 512)
        py = _pool(r3(u), prm['pw'], prm['ps'], tm)
        hp = _mixer_out(hp, py.reshape(bsz * seq, D_POOL), gp, attn.reshape(bsz * seq, D_ATTN), ga,
                        p_prompt[l].reshape(bsz * seq, D_PLE), prm['wo'], prm['wg'], prm['wp'], tm)
        outs[0].append(kv.reshape(bsz, seq, 4, N_KV, HEAD_DIM))
        outs[2].append(win.reshape(bsz, seq, 2, N_KV, HEAD_DIM)[:, seq - min(WINDOW, seq):])
        outs[4].append(r3(u)[:, seq - POOL_BUF:])

        u, gp, qpad, kv, win, ga, gbr, _, _, _, _, _ = _mixer_in(
            hs, tabs_s, prm['ng'], prm['w'], prm['gain'], jm, dec_b)
        ids_s = page_table + l * n_pool
        ck, cv, _ = _compress(ids_s, cache_t, prm['wk'], prm['wv'], prm['pe2'], prm['w2'], True)
        q16 = jnp.pad(qpad.reshape(dec_b, N_HEADS, LANES), ((0, 0), (0, 16 - N_HEADS), (0, 0)))
        ocmp, idx = _nsa_sample_cmp(q16, ck, cv, c2s_s, past)
        gate16 = jnp.pad(gbr[:, :3 * N_HEADS].reshape(dec_b, N_HEADS, 3), ((0, 0), (0, 16 - N_HEADS), (0, LANES - 3)))
        new_rows = jnp.broadcast_to(jnp.concatenate([kv[:, 256:512], win], axis=1)[:, None, :], (dec_b, 8, 512))
        idx = idx.reshape(dec_b, 8, LANES)[:, :N_KV, :N_SELECT].reshape(dec_b, N_KV * N_SELECT)
        o16 = _nsa_sample_attn(ids_s, idx, cache_t, q16, ocmp,
                               gate16, win_t, new_rows, l, past)
        attn = jnp.concatenate([o16[:, h, (h // Q_PER_KV) * HEAD_DIM:(h // Q_PER_KV + 1) * HEAD_DIM]
                                for h in range(N_HEADS)], axis=1)
        u_ext = jnp.concatenate([state_pool[l], u[:, None, :]], axis=1)
        py = _pool(u_ext, prm['pw'], prm['ps'], POOL_BUF + 1)[:, POOL_BUF]
        hs = _mixer_out(hs, py, gp, attn, ga, p_sample[l].reshape(dec_b, D_PLE), prm['wo'], prm['wg'], prm['wp'], dec_b)
        outs[1].append(kv.reshape(dec_b, 1, 4, N_KV, HEAD_DIM))
        outs[3].append(jnp.concatenate([state_win[l][:, 1:], win.reshape(dec_b, 1, 2, N_KV, HEAD_DIM)], axis=1))
        outs[5].append(u_ext[:, 1:])

    return (hp.reshape(bsz, seq, D_MODEL), hs.reshape(dec_b, 1, D_MODEL),
            jnp.stack(outs[0]), jnp.stack(outs[1]), jnp.stack(outs[2]), jnp.stack(outs[3]),
            jnp.stack(outs[4]), jnp.stack(outs[5]))
```

```python
import functools
import math

import jax
import jax.numpy as jnp
from jax import lax
from jax.experimental import pallas as pl
from jax.experimental.pallas import tpu as pltpu

F32 = jnp.float32
BF16 = jnp.bfloat16

D_MODEL = 1024
D_POOL = 512
POOL_WINDOWS = (2, 4, 8, 16)
POOL_GW = 128
POOL_BUF = 15
D_ATTN = 512
HEAD_DIM = 64
N_HEADS = 8
N_KV = 2
Q_PER_KV = 4
KV_W = 128
ROPE_DIM = 16
ROPE_THETA = 500000.0
CMP_LEN = 32
CMP_STRIDE = 16
SEL_LEN = 64
N_SELECT = 16
WINDOW = 512
D_PLE = 256
EPS = 1e-6
BIG = 1e30
NEG = -3.0e38
LANES = 128
D_INR = 2944
PAGE = 128
VMEM_LIMIT = 56 * 1024 * 1024
Q_SCALE = HEAD_DIM ** -0.5 * math.log2(math.e)
SHIFT_SLACK = 60.0


def _dot(a, b):
    return jnp.dot(a, b, preferred_element_type=F32)


def _dot_nt(a, b):
    return lax.dot_general(a, b, (((1,), (1,)), ((), ())), preferred_element_type=F32)


def _split_bf16(x):
    hi = x.astype(BF16)
    return hi, (x - hi.astype(F32)).astype(BF16)


def _silu(x):
    return x * jax.nn.sigmoid(x)


def _mixer_in_kernel(x_ref, cos_ref, s1_ref, s2_ref, ng_ref, w_ref, gain_ref, j_ref,
                     u_ref, gp_ref, q_ref, kv_ref, win_ref, ga_ref, gbr_ref,
                     kaug0_ref, kaug1_ref, vsb_ref, kwb_ref, vwb_ref, *, tm, nper):
    x = x_ref[...]
    ms = jnp.mean(x * x, axis=-1, keepdims=True)
    xn = (x * lax.rsqrt(ms + EPS) * ng_ref[...]).astype(BF16)
    cos, s1, s2 = cos_ref[...], s1_ref[...], s2_ref[...]
    lane = lax.broadcasted_iota(jnp.int32, (tm, LANES), 1)
    lo_half = lane < HEAD_DIM

    def mean_sq(chunks):
        wd = LANES * len(chunks)
        sq = jnp.concatenate([c * c for c in chunks], axis=1).astype(BF16)
        msq = _dot(sq, j_ref[0:wd, 0:wd])
        return [msq[:, i * LANES:(i + 1) * LANES] for i in range(len(chunks))]

    def normrope(zc, msq, gain):
        y = zc * lax.rsqrt(msq + EPS) * gain
        return y * cos + pltpu.roll(y, 8, 1) * s1 + pltpu.roll(y, LANES - 8, 1) * s2

    u_ref[...] = _dot(xn, w_ref[:, 0:512])
    gp_ref[...] = _dot(xn, w_ref[:, 512:1024])
    zq = _dot(xn, w_ref[:, 1024:1536])
    zkv = _dot(xn, w_ref[:, 1536:2304])
    zqc = [zq[:, c * LANES:(c + 1) * LANES] for c in range(4)]
    mq = mean_sq(zqc[0:2]) + mean_sq(zqc[2:4])
    mkc, mks = mean_sq([zkv[:, 0:128], zkv[:, 256:384]])
    mkw, = mean_sq([zkv[:, 512:640]])
    for c in range(4):
        yc = normrope(zqc[c], mq[c], gain_ref[:, c * LANES:(c + 1) * LANES])
        yr = pltpu.roll(yc, HEAD_DIM, 1)
        if c // 2 == 0:
            h0 = jnp.where(lo_half, yc, 0.0)
            h1 = jnp.where(lo_half, yr, 0.0)
        else:
            h0 = jnp.where(lo_half, 0.0, yr)
            h1 = jnp.where(lo_half, 0.0, yc)
        q_ref[:, (2 * c) * LANES:(2 * c + 1) * LANES] = h0.astype(BF16)
        q_ref[:, (2 * c + 1) * LANES:(2 * c + 2) * LANES] = h1.astype(BF16)
    kc = normrope(zkv[:, 0:128], mkc, gain_ref[:, 512:640])
    vc = zkv[:, 128:256]
    ks = normrope(zkv[:, 256:384], mks, gain_ref[:, 640:768])
    vs = zkv[:, 384:512]
    kw = normrope(zkv[:, 512:640], mkw, gain_ref[:, 768:896])
    vw = zkv[:, 640:768]
    kv_ref[:, 0:128] = kc
    kv_ref[:, 128:256] = vc
    kv_ref[:, 256:384] = ks
    kv_ref[:, 384:512] = vs
    win_ref[:, 0:128] = kw
    win_ref[:, 128:256] = vw
    ga_ref[...] = _dot(xn, w_ref[:, 2304:2816])
    gbr_ref[...] = jax.nn.sigmoid(_dot(xn, w_ref[:, 2816:2944]))
    pos = (pl.program_id(0) % nper) * tm + lax.broadcasted_iota(jnp.int32, (tm, LANES), 0)
    ones = jnp.ones((tm, LANES), BF16)
    onehot = jnp.where(lane == pos // SEL_LEN, 1.0, 0.0).astype(BF16)
    kaug0_ref[:, 0:128] = jnp.where(lo_half, ks, jnp.where(lane == HEAD_DIM, 1.0, 0.0)).astype(BF16)
    kaug0_ref[:, 128:256] = onehot
    kaug1_ref[:, 0:128] = jnp.where(lo_half, jnp.where(lane == 0, 1.0, 0.0), ks).astype(BF16)
    kaug1_ref[:, 128:256] = onehot
    vsb_ref[:, 0:128] = vs.astype(BF16)
    vsb_ref[:, 128:256] = ones
    kwb_ref[...] = kw.astype(BF16)
    vwb_ref[:, 0:128] = vw.astype(BF16)
    vwb_ref[:, 128:256] = ones


def _mixer_in(x, tabs, ng, w, gain, jm, tm):
    n = x.shape[0]
    nper = tabs[0].shape[0] // tm
    row = lambda width: pl.BlockSpec((tm, width), lambda i: (i, 0))
    full = lambda a: pl.BlockSpec(a.shape, lambda i: (0,) * a.ndim)
    tab = pl.BlockSpec((tm, LANES), lambda i: (i % nper, 0))
    widths = [(512, F32), (512, F32), (1024, BF16), (512, F32), (256, F32), (512, F32), (128, F32),
              (256, BF16), (256, BF16), (256, BF16), (128, BF16), (256, BF16)]
    return pl.pallas_call(
        functools.partial(_mixer_in_kernel, tm=tm, nper=nper),
        out_shape=[jax.ShapeDtypeStruct((n, wd), dt) for wd, dt in widths],
        grid=(n // tm,),
        in_specs=[row(D_MODEL), tab, tab, tab, full(ng), full(w), full(gain), full(jm)],
        out_specs=[row(wd) for wd, _ in widths],
        compiler_params=pltpu.CompilerParams(dimension_semantics=("arbitrary",),
                                             vmem_limit_bytes=VMEM_LIMIT),
        name="mixer_in",
    )(x, *tabs, ng, w, gain, jm)


def _compress_kernel(pt_ref, src_ref, wk_ref, wv_ref, pe_ref, w2_ref, perm_ref, ck_ref, cv_ref, cvt_ref,
                     *scratch, n_pages, nb, feature_major):
    if feature_major:
        stage_ref, tapk_ref, tapv_ref, bias_ref, sem_ref = scratch
    else:
        buf_ref, bias_ref, sem_ref = scratch
    b = pl.program_id(0)
    nchunk = n_pages * PAGE // CMP_STRIDE
    slot = b % 2

    def page_copies(bb, sl, j):
        if feature_major:
            return [pltpu.make_async_copy(src_ref.at[pt_ref[bb, j], pl.ds(0, 2 * KV_W), :],
                                          stage_ref.at[sl, j], sem_ref.at[sl])]
        return [pltpu.make_async_copy(src_ref.at[pt_ref[bb, j], :, pl.ds(i * KV_W, KV_W)],
                                      buf_ref.at[sl, i, pl.ds(j * PAGE, PAGE), :], sem_ref.at[sl])
                for i in range(2)]

    def start_all(bb, sl):
        def body(j, c):
            for cp in page_copies(bb, sl, j):
                cp.start()
            return c
        lax.fori_loop(0, n_pages, body, 0)

    def wait_all(bb, sl):
        def body(j, c):
            for cp in page_copies(bb, sl, j):
                cp.wait()
            return c
        lax.fori_loop(0, n_pages, body, 0)

    @pl.when(b == 0)
    def _():
        start_all(0, 0)
        acc = jnp.zeros((8, 256), F32)
        for t in range(CMP_STRIDE):
            a0 = jnp.broadcast_to(pe_ref[t:t + 1, :], (8, 256)).astype(BF16)
            a1 = jnp.broadcast_to(pe_ref[CMP_STRIDE + t:CMP_STRIDE + t + 1, :], (8, 256)).astype(BF16)
            pk0 = _dot(a0[:, 0:128], wk_ref[t])
            pk1 = _dot(a1[:, 0:128], wk_ref[t])
            pv0 = _dot(a0[:, 128:256], wv_ref[t])
            pv1 = _dot(a1[:, 128:256], wv_ref[t])
            acc = acc + jnp.concatenate([pk0[:, 0:128] + pk1[:, 128:256],
                                         pv0[:, 0:128] + pv1[:, 128:256]], axis=1)
        bias_ref[...] = acc

    @pl.when(b + 1 < nb)
    def _():
        start_all(b + 1, 1 - slot)

    wait_all(b, slot)

    acc_k = jnp.zeros((nchunk, 256), F32)
    acc_v = jnp.zeros((nchunk, 256), F32)
    if feature_major:
        def regroup(jj, c):
            x2 = jnp.concatenate([stage_ref[slot, 2 * jj], stage_ref[slot, 2 * jj + 1]], axis=1).astype(BF16)
            r0 = pl.multiple_of(jj * CMP_STRIDE, CMP_STRIDE)
            z = _dot_nt(perm_ref[...], x2).astype(BF16)
            for i, tap_ref in enumerate((tapk_ref, tapv_ref)):
                for t in range(CMP_STRIDE):
                    tap_ref[t // 2, pl.ds(r0, CMP_STRIDE), (t % 2) * KV_W:(t % 2 + 1) * KV_W] = (
                        z[t * CMP_STRIDE:(t + 1) * CMP_STRIDE, i * KV_W:(i + 1) * KV_W])
            return c
        lax.fori_loop(0, n_pages // 2, regroup, 0, unroll=8)
        for u in range(CMP_STRIDE // 2):
            acc_k = acc_k + _dot(tapk_ref[u], jnp.concatenate([wk_ref[2 * u], wk_ref[2 * u + 1]], axis=0))
            acc_v = acc_v + _dot(tapv_ref[u], jnp.concatenate([wv_ref[2 * u], wv_ref[2 * u + 1]], axis=0))
    else:
        for t in range(CMP_STRIDE):
            xk = buf_ref[slot, 0, pl.ds(t, nchunk, stride=CMP_STRIDE), :]
            xv = buf_ref[slot, 1, pl.ds(t, nchunk, stride=CMP_STRIDE), :]
            acc_k = acc_k + _dot(xk.astype(BF16), wk_ref[t])
            acc_v = acc_v + _dot(xv.astype(BF16), wv_ref[t])
    row = lax.broadcasted_iota(jnp.int32, (nchunk, LANES), 0)
    valid = row < nchunk - 1
    res = []
    for i, acc in enumerate((acc_k, acc_v)):
        pre = acc[:, 0:128] + pltpu.roll(acc[:, 128:256], nchunk - 1, 0) + bias_ref[0:1, i * 128:(i + 1) * 128]
        hid = _silu(pre).astype(BF16)
        res.append(jnp.where(valid, _dot(hid, w2_ref[i]), 0.0))
    ck_ref[0] = res[0].astype(BF16)
    cv_ref[0] = res[1].astype(BF16)
    cvt_ref[0] = res[1].T.astype(BF16)


def _compress(page_ids, src, wk, wv, pe2, w2, feature_major):
    nb, n_pages = page_ids.shape
    nchunk = n_pages * PAGE // CMP_STRIDE
    full = lambda a: pl.BlockSpec(a.shape, lambda b, pt: (0,) * a.ndim)
    r = jnp.arange(2 * PAGE)
    perm = (r[None, :] == (r[:, None] % CMP_STRIDE) * CMP_STRIDE + r[:, None] // CMP_STRIDE).astype(BF16)
    if feature_major:
        assert n_pages % 2 == 0
        scratch = [pltpu.VMEM((2, n_pages, 2 * KV_W, PAGE), F32),
                   pltpu.VMEM((CMP_STRIDE // 2, nchunk, 2 * KV_W), BF16),
                   pltpu.VMEM((CMP_STRIDE // 2, nchunk, 2 * KV_W), BF16)]
    else:
        scratch = [pltpu.VMEM((2, 2, n_pages * PAGE, KV_W), F32)]
    return pl.pallas_call(
        functools.partial(_compress_kernel, n_pages=n_pages, nb=nb, feature_major=feature_major),
        out_shape=[jax.ShapeDtypeStruct((nb, nchunk, KV_W), BF16), jax.ShapeDtypeStruct((nb, nchunk, KV_W), BF16),
                   jax.ShapeDtypeStruct((nb, KV_W, nchunk), BF16)],
        grid_spec=pltpu.PrefetchScalarGridSpec(
            num_scalar_prefetch=1,
            grid=(nb,),
            in_specs=[pl.BlockSpec(memory_space=pl.ANY), full(wk), full(wv), full(pe2), full(w2), full(perm)],
            out_specs=[pl.BlockSpec((1, nchunk, KV_W), lambda b, pt: (b, 0, 0)),
                       pl.BlockSpec((1, nchunk, KV_W), lambda b, pt: (b, 0, 0)),
                       pl.BlockSpec((1, KV_W, nchunk), lambda b, pt: (b, 0, 0))],
            scratch_shapes=scratch + [pltpu.VMEM((8, 256), F32), pltpu.SemaphoreType.DMA((2,))],
        ),
        compiler_params=pltpu.CompilerParams(dimension_semantics=("arbitrary",),
                                             vmem_limit_bytes=VMEM_LIMIT),
        name="compress",
    )(page_ids, src, wk, wv, pe2, w2, perm)


def _select_topk(x, blk, axis, on_pick):
    nl = float(x.shape[axis])
    blkf = blk.astype(F32)
    for it in range(N_SELECT):
        m = jnp.max(x, axis=axis, keepdims=True)
        idx = jnp.min(jnp.where(x == m, blkf, nl), axis=axis, keepdims=True)
        chosen = blkf == idx
        on_pick(it, idx, chosen, m > -0.5 * BIG)
        x = jnp.where(chosen, NEG, x)


def _importance(imp, blk, cur):
    d = cur - blk
    return jnp.where(blk == 0, BIG, jnp.where(d == 0, BIG, jnp.where(d == 1, BIG, jnp.where(d > 1, imp, -BIG))))


def _nsa_prompt_kernel(q_ref, gbr_ref, ck_ref, cvt_ref, kaug0_ref, kaug1_ref, vs_ref, kw_ref, vw_ref, c2st_ref,
                       o_ref, lhs_ref, acc_ref, bak_ref, c_ref, p0_ref, p1_ref, *, tq, tk):
    t0 = pl.program_id(1) * tq
    rows = Q_PER_KV * tq
    nc = ck_ref.shape[1]
    kaug_refs = (kaug0_ref, kaug1_ref)
    lane_r = lax.broadcasted_iota(jnp.int32, (rows, LANES), 1)
    qpos_r = t0 + lax.broadcasted_iota(jnp.int32, (rows, 1), 0) % tq
    qpos_c = t0 + lax.broadcasted_iota(jnp.int32, (1, rows), 1) % tq
    qpos_t = t0 + lax.broadcasted_iota(jnp.int32, (1, tq), 1)
    blk_t = lax.broadcasted_iota(jnp.int32, (LANES, tq), 0)
    cend = lax.broadcasted_iota(jnp.int32, (nc, 1), 0) * CMP_STRIDE + (CMP_LEN - 1)
    lo_half = lax.broadcasted_iota(jnp.int32, (tq, LANES), 1) < HEAD_DIM
    nwin = WINDOW + tq
    kstart = pl.multiple_of(jnp.maximum(t0 - WINDOW, 0), tq)
    cmask = cend <= qpos_c

    o_cmp, o_win = [], []
    for g in range(N_KV):
        for r in range(Q_PER_KV):
            h = Q_PER_KV * g + r
            lhs_ref[g, r * tq:(r + 1) * tq, 0:LANES] = q_ref[0, :, h * LANES:(h + 1) * LANES]
        qg = lhs_ref[g, :, 0:LANES]

        s = jnp.where(cmask, _dot_nt(ck_ref[0], qg), -BIG)
        m = jnp.max(s, axis=0, keepdims=True)
        e = jnp.exp2(s - m)
        p = e * jnp.where(m > -0.5 * BIG, 1.0 / jnp.sum(e, axis=0, keepdims=True), 0.0)
        o_cmp.append(_dot(cvt_ref[0], p.astype(BF16)).T)
        psum = p[:, 0:tq] + p[:, tq:2 * tq] + p[:, 2 * tq:3 * tq] + p[:, 3 * tq:4 * tq]
        hi, lo = _split_bf16(psum)
        imp = _dot(c2st_ref[...], hi) + _dot(c2st_ref[...], lo)

        bias = [jnp.full((LANES, tq), -BIG, F32)]

        def on_pick(it, idx, chosen, valid):
            bias[0] = jnp.where(chosen, jnp.where(valid, 0.0, -BIG), bias[0])

        _select_topk(_importance(imp, blk_t, qpos_t // SEL_LEN), blk_t, 0, on_pick)
        selbias = bias[0].T.astype(BF16)
        for r in range(Q_PER_KV):
            lhs_ref[g, r * tq:(r + 1) * tq, LANES:2 * LANES] = selbias

        kwpos = kstart + lax.broadcasted_iota(jnp.int32, (1, nwin), 1)
        wmask = (kwpos <= qpos_r) & (kwpos >= qpos_r - WINDOW)
        s = jnp.where(wmask, _dot_nt(qg, kw_ref[0, pl.ds(kstart, nwin), :]), -BIG)
        e = jnp.exp2(s - jnp.max(s, axis=1, keepdims=True))
        ow = _dot(e.astype(BF16), vw_ref[0, pl.ds(kstart, nwin), :])
        o_win.append(ow[:, 0:LANES] / ow[:, LANES:2 * LANES])

    def key_rows(kt):
        return pl.ds(pl.multiple_of(kt * tk, tk), tk)

    def scores(g, kt):
        causal = (kt * tk + lax.broadcasted_iota(jnp.int32, (1, tk), 1)) <= qpos_r
        return jnp.where(causal, _dot_nt(lhs_ref[g], kaug_refs[g][0, key_rows(kt), :]), -BIG)

    def add_pv(g, p, kt):
        acc_ref[g] = acc_ref[g] + _dot(p, vs_ref[0, key_rows(kt), :])

    def recentre(g, kt, p_out, first):
        s_ = scores(g, kt)
        if not first:
            c_old = c_ref[g][:, 0:1]
            s_ = s_ + c_old
        mr = jnp.max(s_, axis=1, keepdims=True)
        c_new = (mr if first else jnp.maximum(c_old, mr)).astype(BF16).astype(F32)
        p_out[g] = jnp.exp2(s_ - c_new).astype(BF16)
        acc_ref[g] = jnp.zeros((rows, 2 * LANES), F32) if first else acc_ref[g] * jnp.exp2(c_old - c_new)
        c_ref[g] = jnp.broadcast_to(c_new, (rows, LANES))
        qf = lhs_ref[g, :, 0:LANES].astype(F32)
        lhs_ref[g, :, 0:LANES] = jnp.where(lane_r == HEAD_DIM * (1 - g), -c_new, qf).astype(BF16)

    for g in range(N_KV):
        recentre(g, 0, p0_ref, True)

    def fast_tile(g, kt, p_in, p_out):
        add_pv(g, p_in[g], kt - 1)
        s_ = scores(g, kt)
        p_out[g] = jnp.exp2(s_).astype(BF16)
        return jnp.max(s_)

    def sel_pair(j, carry):
        a = 2 * j + 1
        for g in range(N_KV):
            bak_ref[g] = acc_ref[g]
        top = [fast_tile(g, a, p0_ref, p1_ref) for g in range(N_KV)]
        top += [fast_tile(g, a + 1, p1_ref, p0_ref) for g in range(N_KV)]

        @pl.when(jnp.logical_not(jnp.maximum(jnp.maximum(top[0], top[1]), jnp.maximum(top[2], top[3])) <= SHIFT_SLACK))
        def _():
            for g in range(N_KV):
                acc_ref[g] = bak_ref[g]
                add_pv(g, jnp.exp2(scores(g, a - 1)).astype(BF16), a - 1)
                recentre(g, a, p1_ref, False)
                add_pv(g, p1_ref[g], a)
                recentre(g, a + 1, p0_ref, False)

        return carry

    n_tiles = (t0 + tq + tk - 1) // tk
    n_pairs = (n_tiles - 1) // 2
    lax.fori_loop(0, n_pairs, sel_pair, 0)
    odd_tail = n_tiles - 1 - 2 * n_pairs == 1

    @pl.when(odd_tail)
    def _():
        last = n_tiles - 1
        top = [fast_tile(g, last, p0_ref, p1_ref) for g in range(N_KV)]

        @pl.when(jnp.logical_not(jnp.maximum(top[0], top[1]) <= SHIFT_SLACK))
        def _():
            for g in range(N_KV):
                recentre(g, last, p1_ref, False)

        for g in range(N_KV):
            add_pv(g, p1_ref[g], last)

    @pl.when(jnp.logical_not(odd_tail))
    def _():
        for g in range(N_KV):
            add_pv(g, p0_ref[g], n_tiles - 1)


    for g in range(N_KV):
        acc = acc_ref[g]
        o_sel = acc[:, 0:LANES] / acc[:, LANES:2 * LANES]
        pieces = []
        for r in range(Q_PER_KV):
            h = Q_PER_KV * g + r
            sl = slice(r * tq, (r + 1) * tq)
            oh = (o_cmp[g][sl] * gbr_ref[0, :, 3 * h:3 * h + 1] + o_sel[sl] * gbr_ref[0, :, 3 * h + 1:3 * h + 2]
                  + o_win[g][sl] * gbr_ref[0, :, 3 * h + 2:3 * h + 3])
            pieces.append(oh if (h % 2) == g else pltpu.roll(oh, HEAD_DIM, 1))
        for c in range(2):
            o_ref[0, :, (2 * g + c) * LANES:(2 * g + c + 1) * LANES] = jnp.where(lo_half, pieces[2 * c], pieces[2 * c + 1])


def _nsa_prompt(qpad, gbr, ck, cvt, kaug0, kaug1, vsb, kwb, vwb, c2st, tq, tk):
    bsz, seq = qpad.shape[:2]
    rows = Q_PER_KV * tq
    per_b = lambda a: pl.BlockSpec((1,) + a.shape[1:], lambda b, i: (b, 0, 0), pipeline_mode=pl.Buffered(1))
    return pl.pallas_call(
        functools.partial(_nsa_prompt_kernel, tq=tq, tk=tk),
        out_shape=jax.ShapeDtypeStruct((bsz, seq, D_ATTN), F32),
        grid=(bsz, seq // tq),
        in_specs=[pl.BlockSpec((1, tq, N_HEADS * LANES), lambda b, i: (b, i, 0)),
                  pl.BlockSpec((1, tq, LANES), lambda b, i: (b, i, 0)),
                  per_b(ck), per_b(cvt), per_b(kaug0), per_b(kaug1), per_b(vsb), per_b(kwb), per_b(vwb),
                  pl.BlockSpec(c2st.shape, lambda b, i: (0, 0))],
        out_specs=pl.BlockSpec((1, tq, D_ATTN), lambda b, i: (b, i, 0)),
        scratch_shapes=[pltpu.VMEM((N_KV, rows, 2 * LANES), BF16),
                        pltpu.VMEM((N_KV, rows, 2 * LANES), F32),
                        pltpu.VMEM((N_KV, rows, 2 * LANES), F32),
                        pltpu.VMEM((N_KV, rows, LANES), F32),
                        pltpu.VMEM((N_KV, rows, tk), BF16),
                        pltpu.VMEM((N_KV, rows, tk), BF16)],
        compiler_params=pltpu.CompilerParams(dimension_semantics=("arbitrary", "arbitrary"),
                                             vmem_limit_bytes=VMEM_LIMIT),
        name="nsa_prompt",
    )(qpad, gbr, ck, cvt, kaug0, kaug1, vsb, kwb, vwb, c2st)


def _nsa_sample_cmp_kernel(q_ref, ck_ref, cv_ref, c2s_ref, ocmp_ref, idx_ref, imp_ref, *, past, nb):
    nc = ck_ref.shape[1]
    nl = c2s_ref.shape[1]
    cend = lax.broadcasted_iota(jnp.int32, (1, nc), 1) * CMP_STRIDE + (CMP_LEN - 1)
    cmask = cend <= past
    row = lax.broadcasted_iota(jnp.int32, (8, 1), 0)

    def one_sequence(b, carry):
        q = q_ref[b]
        s = jnp.where(cmask, _dot_nt(q, ck_ref[b]), -BIG)
        m = jnp.max(s, axis=1, keepdims=True)
        e = jnp.exp2(s - m)
        p = e * jnp.where(m > -0.5 * BIG, 1.0 / jnp.sum(e, axis=1, keepdims=True), 0.0)
        ocmp_ref[b] = _dot(p.astype(BF16), cv_ref[b])
        ps0 = jnp.sum(p[0:Q_PER_KV], axis=0, keepdims=True)
        ps1 = jnp.sum(p[Q_PER_KV:2 * Q_PER_KV], axis=0, keepdims=True)
        psum = jnp.where(row == 0, ps0, jnp.where(row == 1, ps1, 0.0))
        hi, lo = _split_bf16(psum)
        imp_ref[pl.ds(pl.multiple_of(b * 8, 8), 8), :] = _dot(hi, c2s_ref[...]) + _dot(lo, c2s_ref[...])
        return carry

    lax.fori_loop(0, nb, one_sequence, 0)

    blk = lax.broadcasted_iota(jnp.int32, (nb * 8, nl), 1)
    out_lane = lax.broadcasted_iota(jnp.int32, (nb * 8, LANES), 1)
    picks = [jnp.full((nb * 8, LANES), -1, jnp.int32)]

    def on_pick(it, idx, chosen, valid):
        picks[0] = jnp.where(out_lane == it, jnp.where(valid, idx.astype(jnp.int32), -1), picks[0])

    _select_topk(_importance(imp_ref[...], blk, past // SEL_LEN), blk, 1, on_pick)
    idx_ref[...] = picks[0]


def _nsa_sample_cmp(q16, ck, cv, c2s, past):
    bsz = q16.shape[0]
    full = lambda a: pl.BlockSpec(a.shape, lambda i: (0,) * a.ndim)
    return pl.pallas_call(
        functools.partial(_nsa_sample_cmp_kernel, past=past, nb=bsz),
        out_shape=[jax.ShapeDtypeStruct((bsz, 16, LANES), F32), jax.ShapeDtypeStruct((bsz * 8, LANES), jnp.int32)],
        grid=(1,),
        in_specs=[full(q16), full(ck), full(cv), full(c2s)],
        out_specs=[pl.BlockSpec((bsz, 16, LANES), lambda i: (0, 0, 0)),
                   pl.BlockSpec((bsz * 8, LANES), lambda i: (0, 0))],
        scratch_shapes=[pltpu.VMEM((bsz * 8, c2s.shape[1]), F32)],
        compiler_params=pltpu.CompilerParams(dimension_semantics=("arbitrary",),
                                             vmem_limit_bytes=VMEM_LIMIT),
        name="nsa_sample_cmp",
    )(q16, ck, cv, c2s)


def _nsa_sample_attn_kernel(pt_ref, idx_ref, cache_ref, q_ref, ocmp_ref, gate_ref, win_ref, new_ref,
                            o_ref, buf_ref, sem_ref, *, past, nb):
    b = pl.program_id(0)
    nblk = N_KV * N_SELECT
    npast_blk = past // SEL_LEN
    per_page = PAGE // SEL_LEN
    slot = b % 2

    def blk_copy(bb, sl, i):
        j = jnp.clip(idx_ref[bb, i], 0, npast_blk - 1)
        return pltpu.make_async_copy(
            cache_ref.at[pt_ref[bb, j // per_page], pl.ds(2 * KV_W, 2 * KV_W), :],
            buf_ref.at[sl, i // N_SELECT, :, pl.ds(pl.multiple_of((i % N_SELECT) * PAGE, PAGE), PAGE)],
            sem_ref.at[sl])

    def start_all(bb, sl):
        def body(i, c):
            blk_copy(bb, sl, i).start()
            return c
        lax.fori_loop(0, nblk, body, 0)

    def wait_all(bb, sl):
        def body(i, c):
            blk_copy(bb, sl, i).wait()
            return c
        lax.fori_loop(0, nblk, body, 0)

    @pl.when(b == 0)
    def _():
        start_all(0, 0)

    @pl.when(b + 1 < nb)
    def _():
        start_all(b + 1, 1 - slot)

    q = q_ref[0]
    qf = q.astype(F32)
    knew = new_ref[0, 0:1, :]
    ks_new = knew[:, 0:128].astype(BF16).astype(F32)
    vs_new = knew[:, 128:256].astype(BF16).astype(F32)
    kw_new = knew[:, 256:384].astype(BF16).astype(F32)
    vw_new = knew[:, 384:512].astype(BF16).astype(F32)
    row = lax.broadcasted_iota(jnp.int32, (16, 1), 0)

    s = _dot(q, win_ref[0, 0:KV_W, :].astype(BF16))
    s_new = jnp.sum(qf * kw_new, axis=1, keepdims=True)
    m = jnp.maximum(jnp.max(s, axis=1, keepdims=True), s_new)
    e = jnp.exp2(s - m)
    e_new = jnp.exp2(s_new - m)
    o_win = ((_dot_nt(e.astype(BF16), win_ref[0, KV_W:2 * KV_W, :].astype(BF16))
              + e_new.astype(BF16).astype(F32) * vw_new) / (jnp.sum(e, axis=1, keepdims=True) + e_new))

    wait_all(b, slot)

    nkeys = N_SELECT * PAGE
    lane = lax.broadcasted_iota(jnp.int32, (1, nkeys), 1)
    s_new = jnp.sum(qf * ks_new, axis=1, keepdims=True)
    o_sel = jnp.zeros((16, LANES), F32)
    for g in range(N_KV):
        kpos = jnp.full((1, nkeys), past + 1, jnp.int32)
        for n in range(N_SELECT):
            j = idx_ref[b, g * N_SELECT + n]
            ok = (j >= 0) & (j < npast_blk)
            in_blk = (lane // PAGE == n) & ((lane % PAGE) // SEL_LEN == j % per_page) & ok
            kpos = jnp.where(in_blk, (j // per_page) * PAGE + lane % PAGE, kpos)
        s = jnp.where(kpos <= past, _dot(q, buf_ref[slot, g, 0:KV_W, :].astype(BF16)), -BIG)
        m = jnp.maximum(jnp.max(s, axis=1, keepdims=True), s_new)
        e = jnp.exp2(s - m)
        e_new = jnp.exp2(s_new - m)
        og = ((_dot_nt(e.astype(BF16), buf_ref[slot, g, KV_W:2 * KV_W, :].astype(BF16))
               + e_new.astype(BF16).astype(F32) * vs_new) / (jnp.sum(e, axis=1, keepdims=True) + e_new))
        o_sel = jnp.where((row >= g * Q_PER_KV) & (row < (g + 1) * Q_PER_KV), og, o_sel)

    gt = gate_ref[0]
    o_ref[0] = ocmp_ref[0] * gt[:, 0:1] + o_sel * gt[:, 1:2] + o_win * gt[:, 2:3]


def _nsa_sample_attn(page_ids, idx, cache_t, q16, ocmp, gate16, win_t, new_rows, layer, past):
    bsz = q16.shape[0]
    nwin = win_t.shape[3]
    return pl.pallas_call(
        functools.partial(_nsa_sample_attn_kernel, past=past, nb=bsz),
        out_shape=jax.ShapeDtypeStruct((bsz, 16, LANES), F32),
        grid_spec=pltpu.PrefetchScalarGridSpec(
            num_scalar_prefetch=2,
            grid=(bsz,),
            in_specs=[pl.BlockSpec(memory_space=pl.ANY),
                      pl.BlockSpec((1, 16, LANES), lambda b, pt, ix: (b, 0, 0)),
                      pl.BlockSpec((1, 16, LANES), lambda b, pt, ix: (b, 0, 0)),
                      pl.BlockSpec((1, 16, LANES), lambda b, pt, ix: (b, 0, 0)),
                      pl.BlockSpec((None, 1, 2 * KV_W, nwin), lambda b, pt, ix: (layer, b, 0, 0)),
                      pl.BlockSpec((1, 8, 512), lambda b, pt, ix: (b, 0, 0))],
            out_specs=pl.BlockSpec((1, 16, LANES), lambda b, pt, ix: (b, 0, 0)),
            scratch_shapes=[pltpu.VMEM((2, N_KV, 2 * KV_W, N_SELECT * PAGE), F32),
                            pltpu.SemaphoreType.DMA((2,))],
        ),
        compiler_params=pltpu.CompilerParams(dimension_semantics=("arbitrary",)),
        name="nsa_sample_attn",
    )(page_ids, idx, cache_t, q16, ocmp, gate16, win_t, new_rows)


def _pool_kernel(u_ref, halo_ref, pw_ref, ps_ref, y_ref, ext_ref, *, tm):
    i = pl.program_id(1)
    hb = POOL_BUF + 1
    ext_ref[0:hb, :] = jnp.where(i > 0, halo_ref[0], 0.0)
    ext_ref[hb:hb + tm, :] = u_ref[0]
    r = i * tm + lax.broadcasted_iota(jnp.int32, (tm, 1), 0)
    for g, w in enumerate(POOL_WINDOWS):
        cols = slice(g * POOL_GW, (g + 1) * POOL_GW)
        acc = ext_ref[hb:hb + tm, cols]
        for k in range(1, w):
            acc = acc + ext_ref[hb - k:hb - k + tm, cols]
        cnt = jnp.minimum(r + 1, w).astype(F32)
        d = acc / cnt - ext_ref[hb:hb + tm, cols]
        y_ref[0, :, cols] = _dot(d.astype(BF16), pw_ref[g]) * ps_ref[:, cols]


def _pool(u, pw, ps, tm):
    bsz, seq = u.shape[:2]
    hb = POOL_BUF + 1
    ratio = tm // hb
    return pl.pallas_call(
        functools.partial(_pool_kernel, tm=tm),
        out_shape=jax.ShapeDtypeStruct(u.shape, F32),
        grid=(bsz, seq // tm),
        in_specs=[pl.BlockSpec((1, tm, D_POOL), lambda b, i: (b, i, 0)),
                  pl.BlockSpec((1, hb, D_POOL), lambda b, i: (b, jnp.maximum(i * ratio - 1, 0), 0)),
                  pl.BlockSpec(pw.shape, lambda b, i: (0, 0, 0)),
                  pl.BlockSpec(ps.shape, lambda b, i: (0, 0))],
        out_specs=pl.BlockSpec((1, tm, D_POOL), lambda b, i: (b, i, 0)),
        scratch_shapes=[pltpu.VMEM((hb + tm, D_POOL), F32)],
        compiler_params=pltpu.CompilerParams(dimension_semantics=("arbitrary", "arbitrary")),
        name="pool",
    )(u, u, pw, ps)


def _mixer_out_kernel(h_ref, py_ref, gp_ref, at_ref, ga_ref, p_ref, wo_ref, wg_ref, wp_ref, o_ref):
    mp = (py_ref[...] * _silu(gp_ref[...])).astype(BF16)
    ma = (at_ref[...] * _silu(ga_ref[...])).astype(BF16)
    h1 = h_ref[...] + _dot(mp, wo_ref[0:D_POOL, :]) + _dot(ma, wo_ref[D_POOL:D_POOL + D_ATTN, :])
    gate = jax.nn.sigmoid(_dot(h1.astype(BF16), wg_ref[...]))
    o_ref[...] = h1 + gate * _dot(p_ref[...].astype(BF16), wp_ref[...])


def _mixer_out(h, py, gp, at, ga, p, wo, wg, wp, tm):
    n = h.shape[0]
    row = lambda a: pl.BlockSpec((tm, a.shape[1]), lambda i: (i, 0))
    full = lambda a: pl.BlockSpec(a.shape, lambda i: (0, 0))
    return pl.pallas_call(
        _mixer_out_kernel,
        out_shape=jax.ShapeDtypeStruct(h.shape, F32),
        grid=(n // tm,),
        in_specs=[row(h), row(py), row(gp), row(at), row(ga), row(p), full(wo), full(wg), full(wp)],
        out_specs=pl.BlockSpec((tm, D_MODEL), lambda i: (i, 0)),
        compiler_params=pltpu.CompilerParams(dimension_semantics=("arbitrary",),
                                             vmem_limit_bytes=VMEM_LIMIT),
        name="mixer_out",
    )(h, py, gp, at, ga, p, wo, wg, wp)


def _rope_tables(pos):
    half = ROPE_DIM // 2
    inv = ROPE_THETA ** (-jnp.arange(0, ROPE_DIM, 2, dtype=F32) / ROPE_DIM)
    ang = pos.astype(F32)[:, None] * inv[None, :]
    cos, sin = jnp.cos(ang), jnp.sin(ang)
    n = pos.shape[0]
    rest = HEAD_DIM - ROPE_DIM
    c = jnp.concatenate([cos, cos, jnp.ones((n, rest), F32)], axis=1)
    s1 = jnp.concatenate([jnp.zeros((n, half), F32), sin, jnp.zeros((n, rest), F32)], axis=1)
    s2 = jnp.concatenate([-sin, jnp.zeros((n, half + rest), F32)], axis=1)
    return tuple(jnp.tile(t, (1, LANES // HEAD_DIM)) for t in (c, s1, s2))


def _cmp_to_sel(nc, n_cmp, nl):
    ci = jnp.arange(nc)[:, None] * CMP_STRIDE
    sj = jnp.arange(nl)[None, :] * SEL_LEN
    m = (ci < sj + SEL_LEN) & (ci + CMP_LEN > sj) & (jnp.arange(nc)[:, None] < n_cmp)
    return m.astype(BF16)


def _layer_params(l, norm_g, w_in, q_norm, k_norm, cmp_pe, cmp_w1, cmp_w2, pool_w, pool_scale, w_out,
                  ple_gate, ple_proj):
    w = w_in[l]
    pad = jnp.zeros((D_MODEL, D_INR - 2840), w.dtype)
    w_r = jnp.concatenate([w[:, 0:2304], w[:, 2328:2840], w[:, 2304:2328], pad], axis=1).astype(BF16)
    ones = jnp.ones((KV_W,), F32)
    gain = jnp.concatenate([jnp.tile(q_norm[l], N_HEADS) * Q_SCALE,
                            jnp.tile(k_norm[l, 0], N_KV), jnp.tile(k_norm[l, 1], N_KV),
                            jnp.tile(k_norm[l, 2], N_KV), ones, ones, ones])[None, :].astype(F32)
    eye = jnp.eye(N_KV, dtype=F32)

    def taps(w1):
        w1r = w1.reshape(2, CMP_STRIDE, HEAD_DIM, HEAD_DIM)
        return jnp.einsum('gh,atde->tgdahe', eye, w1r).reshape(CMP_STRIDE, KV_W, 2 * KV_W).astype(BF16)

    w2 = jnp.stack([jnp.kron(eye, cmp_w2[l, i]) for i in range(2)]).astype(BF16)
    pe2 = jnp.concatenate([jnp.tile(cmp_pe[l, 0], (1, N_KV)), jnp.tile(cmp_pe[l, 1], (1, N_KV))], axis=1)
    return dict(ng=norm_g[l][None, :], w=w_r, gain=gain, wk=taps(cmp_w1[l, 0]), wv=taps(cmp_w1[l, 1]),
                w2=w2, pe2=pe2, pw=pool_w[l].astype(BF16), ps=pool_scale[l][None, :],
                wo=w_out[l].astype(BF16), wg=ple_gate[l].astype(BF16), wp=ple_proj[l].astype(BF16))


def kernel(x_prompt, x_sample, cache_kv, state_win, state_pool, page_table, p_prompt, p_sample,
           norm_g, w_in, q_norm, k_norm, cmp_pe, cmp_w1, cmp_w2, pool_w, pool_scale, w_out,
           ple_gate, ple_proj):
    depth = w_in.shape[0]
    bsz, seq, _ = x_prompt.shape
    dec_b, n_pages = page_table.shape
    n_pool, page_size = cache_kv.shape[1:3]
    past = n_pages * page_size
    n_win = state_win.shape[2]
    assert page_size == PAGE and seq % 512 == 0 and seq >= WINDOW + 128 and past >= WINDOW
    assert x_sample.shape[1] == 1 and past % SEL_LEN == 0 and n_win % LANES == 0
    n_sel_p = seq // SEL_LEN
    n_sel_s = past // SEL_LEN + 1
    assert N_SELECT <= n_sel_p <= LANES
    nl_s = -(-n_sel_s // LANES) * LANES
    nc_p, nc_s = seq // CMP_STRIDE, past // CMP_STRIDE

    tm = 256
    tabs_p = _rope_tables(jnp.arange(seq))
    tabs_s = _rope_tables(jnp.full((dec_b,), past, jnp.int32))
    jm = jnp.kron(jnp.eye(2 * LANES // HEAD_DIM, dtype=F32), jnp.full((HEAD_DIM, HEAD_DIM), 1.0 / HEAD_DIM, F32)).astype(BF16)
    c2st_p = _cmp_to_sel(nc_p, nc_p - 1, LANES).T
    c2s_s = _cmp_to_sel(nc_s, nc_s - 1, nl_s)
    ids_p = jnp.arange(bsz * seq // PAGE, dtype=jnp.int32).reshape(bsz, seq // PAGE)
    cache_t = jnp.transpose(cache_kv, (0, 1, 3, 4, 5, 2)).reshape(depth * n_pool, 4 * KV_W, PAGE)
    win_t = jnp.transpose(state_win, (0, 1, 3, 4, 5, 2)).reshape(depth, dec_b, 2 * KV_W, n_win)

    hp = x_prompt.reshape(bsz * seq, D_MODEL)
    hs = x_sample.reshape(dec_b, D_MODEL)
    outs = [[] for _ in range(6)]
    for l in range(depth):
        prm = _layer_params(l, norm_g, w_in, q_norm, k_norm, cmp_pe, cmp_w1, cmp_w2, pool_w, pool_scale,
                            w_out, ple_gate, ple_proj)
        u, gp, qpad, kv, win, ga, gbr, kaug0, kaug1, vsb, kwb, vwb = _mixer_in(
            hp, tabs_p, prm['ng'], prm['w'], prm['gain'], jm, tm)
        ck, _, cvt = _compress(ids_p, kv.reshape(bsz * seq // PAGE, PAGE, 512), prm['wk'], prm['wv'], prm['pe2'],
                               prm['w2'], False)
        r3 = lambda a: a.reshape(bsz, seq, a.shape[-1])
        attn = _nsa_prompt(r3(qpad), r3(gbr), ck, cvt, r3(kaug0), r3(kaug1), r3(vsb), r3(kwb), r3(vwb), c2st_p, 256, 512)
        py = _pool(r3(u), prm['pw'], prm['ps'], tm)
        hp = _mixer_out(hp, py.reshape(bsz * seq, D_POOL), gp, attn.reshape(bsz * seq, D_ATTN), ga,
                        p_prompt[l].reshape(bsz * seq, D_PLE), prm['wo'], prm['wg'], prm['wp'], tm)
        outs[0].append(kv.reshape(bsz, seq, 4, N_KV, HEAD_DIM))
        outs[2].append(win.reshape(bsz, seq, 2, N_KV, HEAD_DIM)[:, seq - min(WINDOW, seq):])
        outs[4].append(r3(u)[:, seq - POOL_BUF:])

        u, gp, qpad, kv, win, ga, gbr, _, _, _, _, _ = _mixer_in(
            hs, tabs_s, prm['ng'], prm['w'], prm['gain'], jm, dec_b)
        ids_s = page_table + l * n_pool
        ck, cv, _ = _compress(ids_s, cache_t, prm['wk'], prm['wv'], prm['pe2'], prm['w2'], True)
        q16 = jnp.pad(qpad.reshape(dec_b, N_HEADS, LANES), ((0, 0), (0, 16 - N_HEADS), (0, 0)))
        ocmp, idx = _nsa_sample_cmp(q16, ck, cv, c2s_s, past)
        gate16 = jnp.pad(gbr[:, :3 * N_HEADS].reshape(dec_b, N_HEADS, 3), ((0, 0), (0, 16 - N_HEADS), (0, LANES - 3)))
        new_rows = jnp.broadcast_to(jnp.concatenate([kv[:, 256:512], win], axis=1)[:, None, :], (dec_b, 8, 512))
        idx = idx.reshape(dec_b, 8, LANES)[:, :N_KV, :N_SELECT].reshape(dec_b, N_KV * N_SELECT)
        o16 = _nsa_sample_attn(ids_s, idx, cache_t, q16, ocmp,
                               gate16, win_t, new_rows, l, past)
        attn = jnp.concatenate([o16[:, h, (h // Q_PER_KV) * HEAD_DIM:(h // Q_PER_KV + 1) * HEAD_DIM]
                                for h in range(N_HEADS)], axis=1)
        u_ext = jnp.concatenate([state_pool[l], u[:, None, :]], axis=1)
        py = _pool(u_ext, prm['pw'], prm['ps'], POOL_BUF + 1)[:, POOL_BUF]
        hs = _mixer_out(hs, py, gp, attn, ga, p_sample[l].reshape(dec_b, D_PLE), prm['wo'], prm['wg'], prm['wp'], dec_b)
        outs[1].append(kv.reshape(dec_b, 1, 4, N_KV, HEAD_DIM))
        outs[3].append(jnp.concatenate([state_win[l][:, 1:], win.reshape(dec_b, 1, 2, N_KV, HEAD_DIM)], axis=1))
        outs[5].append(u_ext[:, 1:])

    return (hp.reshape(bsz, seq, D_MODEL), hs.reshape(dec_b, 1, D_MODEL),
            jnp.stack(outs[0]), jnp.stack(outs[1]), jnp.stack(outs[2]), jnp.stack(outs[3]),
            jnp.stack(outs[4]), jnp.stack(outs[5]))
```

```python
import functools
import math

import jax
import jax.numpy as jnp
from jax import lax
from jax.experimental import pallas as pl
from jax.experimental.pallas import tpu as pltpu

F32 = jnp.float32
BF16 = jnp.bfloat16

D_MODEL = 1024
D_POOL = 512
POOL_WINDOWS = (2, 4, 8, 16)
POOL_GW = 128
POOL_BUF = 15
D_ATTN = 512
HEAD_DIM = 64
N_HEADS = 8
N_KV = 2
Q_PER_KV = 4
KV_W = 128
ROPE_DIM = 16
ROPE_THETA = 500000.0
CMP_LEN = 32
CMP_STRIDE = 16
SEL_LEN = 64
N_SELECT = 16
WINDOW = 512
D_PLE = 256
EPS = 1e-6
BIG = 1e30
NEG = -3.0e38
LANES = 128
D_INR = 2944
PAGE = 128
VMEM_LIMIT = 56 * 1024 * 1024
Q_SCALE = HEAD_DIM ** -0.5 * math.log2(math.e)
SHIFT_SLACK = 60.0


def _dot(a, b):
    return jnp.dot(a, b, preferred_element_type=F32)


def _dot_nt(a, b):
    return lax.dot_general(a, b, (((1,), (1,)), ((), ())), preferred_element_type=F32)


def _split_bf16(x):
    hi = x.astype(BF16)
    return hi, (x - hi.astype(F32)).astype(BF16)


def _silu(x):
    return x * jax.nn.sigmoid(x)


def _mixer_in_kernel(x_ref, cos_ref, s1_ref, s2_ref, ng_ref, w_ref, gain_ref, j_ref,
                     u_ref, gp_ref, q_ref, kv_ref, win_ref, ga_ref, gbr_ref,
                     kaug0_ref, kaug1_ref, vsb_ref, kwb_ref, vwb_ref, *, tm, nper):
    x = x_ref[...]
    ms = jnp.mean(x * x, axis=-1, keepdims=True)
    xn = (x * lax.rsqrt(ms + EPS) * ng_ref[...]).astype(BF16)
    cos, s1, s2 = cos_ref[...], s1_ref[...], s2_ref[...]
    lane = lax.broadcasted_iota(jnp.int32, (tm, LANES), 1)
    lo_half = lane < HEAD_DIM

    def mean_sq(chunks):
        wd = LANES * len(chunks)
        sq = jnp.concatenate([c * c for c in chunks], axis=1).astype(BF16)
        msq = _dot(sq, j_ref[0:wd, 0:wd])
        return [msq[:, i * LANES:(i + 1) * LANES] for i in range(len(chunks))]

    def normrope(zc, msq, gain):
        y = zc * lax.rsqrt(msq + EPS) * gain
        return y * cos + pltpu.roll(y, 8, 1) * s1 + pltpu.roll(y, LANES - 8, 1) * s2

    u_ref[...] = _dot(xn, w_ref[:, 0:512])
    gp_ref[...] = _dot(xn, w_ref[:, 512:1024])
    zq = _dot(xn, w_ref[:, 1024:1536])
    zkv = _dot(xn, w_ref[:, 1536:2304])
    zqc = [zq[:, c * LANES:(c + 1) * LANES] for c in range(4)]
    mq = mean_sq(zqc[0:2]) + mean_sq(zqc[2:4])
    mkc, mks = mean_sq([zkv[:, 0:128], zkv[:, 256:384]])
    mkw, = mean_sq([zkv[:, 512:640]])
    for c in range(4):
        yc = normrope(zqc[c], mq[c], gain_ref[:, c * LANES:(c + 1) * LANES])
        yr = pltpu.roll(yc, HEAD_DIM, 1)
        if c // 2 == 0:
            h0 = jnp.where(lo_half, yc, 0.0)
            h1 = jnp.where(lo_half, yr, 0.0)
        else:
            h0 = jnp.where(lo_half, 0.0, yr)
            h1 = jnp.where(lo_half, 0.0, yc)
        q_ref[:, (2 * c) * LANES:(2 * c + 1) * LANES] = h0.astype(BF16)
        q_ref[:, (2 * c + 1) * LANES:(2 * c + 2) * LANES] = h1.astype(BF16)
    kc = normrope(zkv[:, 0:128], mkc, gain_ref[:, 512:640])
    vc = zkv[:, 128:256]
    ks = normrope(zkv[:, 256:384], mks, gain_ref[:, 640:768])
    vs = zkv[:, 384:512]
    kw = normrope(zkv[:, 512:640], mkw, gain_ref[:, 768:896])
    vw = zkv[:, 640:768]
    kv_ref[:, 0:128] = kc
    kv_ref[:, 128:256] = vc
    kv_ref[:, 256:384] = ks
    kv_ref[:, 384:512] = vs
    win_ref[:, 0:128] = kw
    win_ref[:, 128:256] = vw
    ga_ref[...] = _dot(xn, w_ref[:, 2304:2816])
    gbr_ref[...] = jax.nn.sigmoid(_dot(xn, w_ref[:, 2816:2944]))
    pos = (pl.program_id(0) % nper) * tm + lax.broadcasted_iota(jnp.int32, (tm, LANES), 0)
    ones = jnp.ones((tm, LANES), BF16)
    onehot = jnp.where(lane == pos // SEL_LEN, 1.0, 0.0).astype(BF16)
    kaug0_ref[:, 0:128] = jnp.where(lo_half, ks, jnp.where(lane == HEAD_DIM, 1.0, 0.0)).astype(BF16)
    kaug0_ref[:, 128:256] = onehot
    kaug1_ref[:, 0:128] = jnp.where(lo_half, jnp.where(lane == 0, 1.0, 0.0), ks).astype(BF16)
    kaug1_ref[:, 128:256] = onehot
    vsb_ref[:, 0:128] = vs.astype(BF16)
    vsb_ref[:, 128:256] = ones
    kwb_ref[...] = kw.astype(BF16)
    vwb_ref[:, 0:128] = vw.astype(BF16)
    vwb_ref[:, 128:256] = ones


def _mixer_in(x, tabs, ng, w, gain, jm, tm):
    n = x.shape[0]
    nper = tabs[0].shape[0] // tm
    row = lambda width: pl.BlockSpec((tm, width), lambda i: (i, 0))
    full = lambda a: pl.BlockSpec(a.shape, lambda i: (0,) * a.ndim)
    tab = pl.BlockSpec((tm, LANES), lambda i: (i % nper, 0))
    widths = [(512, F32), (512, F32), (1024, BF16), (512, F32), (256, F32), (512, F32), (128, F32),
              (256, BF16), (256, BF16), (256, BF16), (128, BF16), (256, BF16)]
    return pl.pallas_call(
        functools.partial(_mixer_in_kernel, tm=tm, nper=nper),
        out_shape=[jax.ShapeDtypeStruct((n, wd), dt) for wd, dt in widths],
        grid=(n // tm,),
        in_specs=[row(D_MODEL), tab, tab, tab, full(ng), full(w), full(gain), full(jm)],
        out_specs=[row(wd) for wd, _ in widths],
        compiler_params=pltpu.CompilerParams(dimension_semantics=("arbitrary",),
                                             vmem_limit_bytes=VMEM_LIMIT),
        name="mixer_in",
    )(x, *tabs, ng, w, gain, jm)


def _compress_kernel(pt_ref, src_ref, wk_ref, wv_ref, pe_ref, w2_ref, perm_ref, ck_ref, cv_ref, cvt_ref,
                     *scratch, n_pages, nb, feature_major):
    if feature_major:
        stage_ref, tapk_ref, tapv_ref, bias_ref, sem_ref = scratch
    else:
        buf_ref, bias_ref, sem_ref = scratch
    b = pl.program_id(0)
    nchunk = n_pages * PAGE // CMP_STRIDE
    slot = b % 2

    def page_copies(bb, sl, j):
        if feature_major:
            return [pltpu.make_async_copy(src_ref.at[pt_ref[bb, j], pl.ds(0, 2 * KV_W), :],
                                          stage_ref.at[sl, j], sem_ref.at[sl])]
        return [pltpu.make_async_copy(src_ref.at[pt_ref[bb, j], :, pl.ds(i * KV_W, KV_W)],
                                      buf_ref.at[sl, i, pl.ds(j * PAGE, PAGE), :], sem_ref.at[sl])
                for i in range(2)]

    def start_all(bb, sl):
        def body(j, c):
            for cp in page_copies(bb, sl, j):
                cp.start()
            return c
        lax.fori_loop(0, n_pages, body, 0)

    def wait_all(bb, sl):
        def body(j, c):
            for cp in page_copies(bb, sl, j):
                cp.wait()
            return c
        lax.fori_loop(0, n_pages, body, 0)

    @pl.when(b == 0)
    def _():
        start_all(0, 0)
        acc = jnp.zeros((8, 256), F32)
        for t in range(CMP_STRIDE):
            a0 = jnp.broadcast_to(pe_ref[t:t + 1, :], (8, 256)).astype(BF16)
            a1 = jnp.broadcast_to(pe_ref[CMP_STRIDE + t:CMP_STRIDE + t + 1, :], (8, 256)).astype(BF16)
            pk0 = _dot(a0[:, 0:128], wk_ref[t])
            pk1 = _dot(a1[:, 0:128], wk_ref[t])
            pv0 = _dot(a0[:, 128:256], wv_ref[t])
            pv1 = _dot(a1[:, 128:256], wv_ref[t])
            acc = acc + jnp.concatenate([pk0[:, 0:128] + pk1[:, 128:256],
                                         pv0[:, 0:128] + pv1[:, 128:256]], axis=1)
        bias_ref[...] = acc

    @pl.when(b + 1 < nb)
    def _():
        start_all(b + 1, 1 - slot)

    wait_all(b, slot)

    acc_k = jnp.zeros((nchunk, 256), F32)
    acc_v = jnp.zeros((nchunk, 256), F32)
    if feature_major:
        def regroup(jj, c):
            x2 = jnp.concatenate([stage_ref[slot, 2 * jj], stage_ref[slot, 2 * jj + 1]], axis=1).astype(BF16)
            r0 = pl.multiple_of(jj * CMP_STRIDE, CMP_STRIDE)
            z = _dot_nt(perm_ref[...], x2).astype(BF16)
            for i, tap_ref in enumerate((tapk_ref, tapv_ref)):
                for t in range(CMP_STRIDE):
                    tap_ref[t // 2, pl.ds(r0, CMP_STRIDE), (t % 2) * KV_W:(t % 2 + 1) * KV_W] = (
                        z[t * CMP_STRIDE:(t + 1) * CMP_STRIDE, i * KV_W:(i + 1) * KV_W])
            return c
        lax.fori_loop(0, n_pages // 2, regroup, 0, unroll=8)
        for u in range(CMP_STRIDE // 2):
            acc_k = acc_k + _dot(tapk_ref[u], jnp.concatenate([wk_ref[2 * u], wk_ref[2 * u + 1]], axis=0))
            acc_v = acc_v + _dot(tapv_ref[u], jnp.concatenate([wv_ref[2 * u], wv_ref[2 * u + 1]], axis=0))
    else:
        for t in range(CMP_STRIDE):
            xk = buf_ref[slot, 0, pl.ds(t, nchunk, stride=CMP_STRIDE), :]
            xv = buf_ref[slot, 1, pl.ds(t, nchunk, stride=CMP_STRIDE), :]
            acc_k = acc_k + _dot(xk.astype(BF16), wk_ref[t])
            acc_v = acc_v + _dot(xv.astype(BF16), wv_ref[t])
    row = lax.broadcasted_iota(jnp.int32, (nchunk, LANES), 0)
    valid = row < nchunk - 1
    res = []
    for i, acc in enumerate((acc_k, acc_v)):
        pre = acc[:, 0:128] + pltpu.roll(acc[:, 128:256], nchunk - 1, 0) + bias_ref[0:1, i * 128:(i + 1) * 128]
        hid = _silu(pre).astype(BF16)
        res.append(jnp.where(valid, _dot(hid, w2_ref[i]), 0.0))
    ck_ref[0] = res[0].astype(BF16)
    cv_ref[0] = res[1].astype(BF16)
    cvt_ref[0] = res[1].T.astype(BF16)


def _compress(page_ids, src, wk, wv, pe2, w2, feature_major):
    nb, n_pages = page_ids.shape
    nchunk = n_pages * PAGE // CMP_STRIDE
    full = lambda a: pl.BlockSpec(a.shape, lambda b, pt: (0,) * a.ndim)
    r = jnp.arange(2 * PAGE)
    perm = (r[None, :] == (r[:, None] % CMP_STRIDE) * CMP_STRIDE + r[:, None] // CMP_STRIDE).astype(BF16)
    if feature_major:
        assert n_pages % 2 == 0
        scratch = [pltpu.VMEM((2, n_pages, 2 * KV_W, PAGE), F32),
                   pltpu.VMEM((CMP_STRIDE // 2, nchunk, 2 * KV_W), BF16),
                   pltpu.VMEM((CMP_STRIDE // 2, nchunk, 2 * KV_W), BF16)]
    else:
        scratch = [pltpu.VMEM((2, 2, n_pages * PAGE, KV_W), F32)]
    return pl.pallas_call(
        functools.partial(_compress_kernel, n_pages=n_pages, nb=nb, feature_major=feature_major),
        out_shape=[jax.ShapeDtypeStruct((nb, nchunk, KV_W), BF16), jax.ShapeDtypeStruct((nb, nchunk, KV_W), BF16),
                   jax.ShapeDtypeStruct((nb, KV_W, nchunk), BF16)],
        grid_spec=pltpu.PrefetchScalarGridSpec(
            num_scalar_prefetch=1,
            grid=(nb,),
            in_specs=[pl.BlockSpec(memory_space=pl.ANY), full(wk), full(wv), full(pe2), full(w2), full(perm)],
            out_specs=[pl.BlockSpec((1, nchunk, KV_W), lambda b, pt: (b, 0, 0)),
                       pl.BlockSpec((1, nchunk, KV_W), lambda b, pt: (b, 0, 0)),
                       pl.BlockSpec((1, KV_W, nchunk), lambda b, pt: (b, 0, 0))],
            scratch_shapes=scratch + [pltpu.VMEM((8, 256), F32), pltpu.SemaphoreType.DMA((2,))],
        ),
        compiler_params=pltpu.CompilerParams(dimension_semantics=("arbitrary",),
                                             vmem_limit_bytes=VMEM_LIMIT),
        name="compress",
    )(page_ids, src, wk, wv, pe2, w2, perm)


def _select_topk(x, blk, axis, on_pick):
    nl = float(x.shape[axis])
    blkf = blk.astype(F32)
    for it in range(N_SELECT):
        m = jnp.max(x, axis=axis, keepdims=True)
        idx = jnp.min(jnp.where(x == m, blkf, nl), axis=axis, keepdims=True)
        chosen = blkf == idx
        on_pick(it, idx, chosen, m > -0.5 * BIG)
        x = jnp.where(chosen, NEG, x)


def _importance(imp, blk, cur):
    d = cur - blk
    return jnp.where(blk == 0, BIG, jnp.where(d == 0, BIG, jnp.where(d == 1, BIG, jnp.where(d > 1, imp, -BIG))))


def _nsa_prompt_kernel(q_ref, gbr_ref, ck_ref, cvt_ref, kaug0_ref, kaug1_ref, vs_ref, kw_ref, vw_ref, c2st_ref,
                       o_ref, lhs_ref, acc_ref, bak_ref, c_ref, p0_ref, p1_ref, *, tq, tk):
    t0 = pl.program_id(1) * tq
    rows = Q_PER_KV * tq
    nc = ck_ref.shape[1]
    kaug_refs = (kaug0_ref, kaug1_ref)
    lane_r = lax.broadcasted_iota(jnp.int32, (rows, LANES), 1)
    qpos_r = t0 + lax.broadcasted_iota(jnp.int32, (rows, 1), 0) % tq
    qpos_c = t0 + lax.broadcasted_iota(jnp.int32, (1, rows), 1) % tq
    qpos_t = t0 + lax.broadcasted_iota(jnp.int32, (1, tq), 1)
    blk_t = lax.broadcasted_iota(jnp.int32, (LANES, tq), 0)
    cend = lax.broadcasted_iota(jnp.int32, (nc, 1), 0) * CMP_STRIDE + (CMP_LEN - 1)
    lo_half = lax.broadcasted_iota(jnp.int32, (tq, LANES), 1) < HEAD_DIM
    nwin = WINDOW + tq
    kstart = pl.multiple_of(jnp.maximum(t0 - WINDOW, 0), tq)
    cmask = cend <= qpos_c

    o_cmp, o_win = [], []
    for g in range(N_KV):
        for r in range(Q_PER_KV):
            h = Q_PER_KV * g + r
            lhs_ref[g, r * tq:(r + 1) * tq, 0:LANES] = q_ref[0, :, h * LANES:(h + 1) * LANES]
        qg = lhs_ref[g, :, 0:LANES]

        s = jnp.where(cmask, _dot_nt(ck_ref[0], qg), -BIG)
        m = jnp.max(s, axis=0, keepdims=True)
        e = jnp.exp2(s - m)
        p = e * jnp.where(m > -0.5 * BIG, 1.0 / jnp.sum(e, axis=0, keepdims=True), 0.0)
        o_cmp.append(_dot(cvt_ref[0], p.astype(BF16)).T)
        psum = p[:, 0:tq] + p[:, tq:2 * tq] + p[:, 2 * tq:3 * tq] + p[:, 3 * tq:4 * tq]
        hi, lo = _split_bf16(psum)
        imp = _dot(c2st_ref[...], hi) + _dot(c2st_ref[...], lo)

        bias = [jnp.full((LANES, tq), -BIG, F32)]

        def on_pick(it, idx, chosen, valid):
            bias[0] = jnp.where(chosen, jnp.where(valid, 0.0, -BIG), bias[0])

        _select_topk(_importance(imp, blk_t, qpos_t // SEL_LEN), blk_t, 0, on_pick)
        selbias = bias[0].T.astype(BF16)
        for r in range(Q_PER_KV):
            lhs_ref[g, r * tq:(r + 1) * tq, LANES:2 * LANES] = selbias

        kwpos = kstart + lax.broadcasted_iota(jnp.int32, (1, nwin), 1)
        wmask = (kwpos <= qpos_r) & (kwpos >= qpos_r - WINDOW)
        s = jnp.where(wmask, _dot_nt(qg, kw_ref[0, pl.ds(kstart, nwin), :]), -BIG)
        e = jnp.exp2(s - jnp.max(s, axis=1, keepdims=True))
        ow = _dot(e.astype(BF16), vw_ref[0, pl.ds(kstart, nwin), :])
        o_win.append(ow[:, 0:LANES] / ow[:, LANES:2 * LANES])

    def key_rows(kt):
        return pl.ds(pl.multiple_of(kt * tk, tk), tk)

    def scores(g, kt):
        causal = (kt * tk + lax.broadcasted_iota(jnp.int32, (1, tk), 1)) <= qpos_r
        return jnp.where(causal, _dot_nt(lhs_ref[g], kaug_refs[g][0, key_rows(kt), :]), -BIG)

    def add_pv(g, p, kt):
        acc_ref[g] = acc_ref[g] + _dot(p, vs_ref[0, key_rows(kt), :])

    def recentre(g, kt, p_out, first):
        s_ = scores(g, kt)
        if not first:
            c_old = c_ref[g][:, 0:1]
            s_ = s_ + c_old
        mr = jnp.max(s_, axis=1, keepdims=True)
        c_new = (mr if first else jnp.maximum(c_old, mr)).astype(BF16).astype(F32)
        p_out[g] = jnp.exp2(s_ - c_new).astype(BF16)
        acc_ref[g] = jnp.zeros((rows, 2 * LANES), F32) if first else acc_ref[g] * jnp.exp2(c_old - c_new)
        c_ref[g] = jnp.broadcast_to(c_new, (rows, LANES))
        qf = lhs_ref[g, :, 0:LANES].astype(F32)
        lhs_ref[g, :, 0:LANES] = jnp.where(lane_r == HEAD_DIM * (1 - g), -c_new, qf).astype(BF16)

    for g in range(N_KV):
        recentre(g, 0, p0_ref, True)

    def fast_tile(g, kt, p_in, p_out):
        add_pv(g, p_in[g], kt - 1)
        s_ = scores(g, kt)
        p_out[g] = jnp.exp2(s_).astype(BF16)
        return jnp.max(s_)

    def sel_pair(j, carry):
        a = 2 * j + 1
        for g in range(N_KV):
            bak_ref[g] = acc_ref[g]
        top = [fast_tile(g, a, p0_ref, p1_ref) for g in range(N_KV)]
        top += [fast_tile(g, a + 1, p1_ref, p0_ref) for g in range(N_KV)]

        @pl.when(jnp.logical_not(jnp.maximum(jnp.maximum(top[0], top[1]), jnp.maximum(top[2], top[3])) <= SHIFT_SLACK))
        def _():
            for g in range(N_KV):
                acc_ref[g] = bak_ref[g]
                add_pv(g, jnp.exp2(scores(g, a - 1)).astype(BF16), a - 1)
                recentre(g, a, p1_ref, False)
                add_pv(g, p1_ref[g], a)
                recentre(g, a + 1, p0_ref, False)

        return carry

    n_tiles = (t0 + tq + tk - 1) // tk
    n_pairs = (n_tiles - 1) // 2
    lax.fori_loop(0, n_pairs, sel_pair, 0)
    odd_tail = n_tiles - 1 - 2 * n_pairs == 1

    @pl.when(odd_tail)
    def _():
        last = n_tiles - 1
        top = [fast_tile(g, last, p0_ref, p1_ref) for g in range(N_KV)]

        @pl.when(jnp.logical_not(jnp.maximum(top[0], top[1]) <= SHIFT_SLACK))
        def _():
            for g in range(N_KV):
                recentre(g, last, p1_ref, False)

        for g in range(N_KV):
            add_pv(g, p1_ref[g], last)

    @pl.when(jnp.logical_not(odd_tail))
    def _():
        for g in range(N_KV):
            add_pv(g, p0_ref[g], n_tiles - 1)


    for g in range(N_KV):
        acc = acc_ref[g]
        o_sel = acc[:, 0:LANES] / acc[:, LANES:2 * LANES]
        pieces = []
        for r in range(Q_PER_KV):
            h = Q_PER_KV * g + r
            sl = slice(r * tq, (r + 1) * tq)
            oh = (o_cmp[g][sl] * gbr_ref[0, :, 3 * h:3 * h + 1] + o_sel[sl] * gbr_ref[0, :, 3 * h + 1:3 * h + 2]
                  + o_win[g][sl] * gbr_ref[0, :, 3 * h + 2:3 * h + 3])
            pieces.append(oh if (h % 2) == g else pltpu.roll(oh, HEAD_DIM, 1))
        for c in range(2):
            o_ref[0, :, (2 * g + c) * LANES:(2 * g + c + 1) * LANES] = jnp.where(lo_half, pieces[2 * c], pieces[2 * c + 1])


def _nsa_prompt(qpad, gbr, ck, cvt, kaug0, kaug1, vsb, kwb, vwb, c2st, tq, tk):
    bsz, seq = qpad.shape[:2]
    rows = Q_PER_KV * tq
    per_b = lambda a: pl.BlockSpec((1,) + a.shape[1:], lambda b, i: (b, 0, 0), pipeline_mode=pl.Buffered(1))
    return pl.pallas_call(
        functools.partial(_nsa_prompt_kernel, tq=tq, tk=tk),
        out_shape=jax.ShapeDtypeStruct((bsz, seq, D_ATTN), F32),
        grid=(bsz, seq // tq),
        in_specs=[pl.BlockSpec((1, tq, N_HEADS * LANES), lambda b, i: (b, i, 0)),
                  pl.BlockSpec((1, tq, LANES), lambda b, i: (b, i, 0)),
                  per_b(ck), per_b(cvt), per_b(kaug0), per_b(kaug1), per_b(vsb), per_b(kwb), per_b(vwb),
                  pl.BlockSpec(c2st.shape, lambda b, i: (0, 0))],
        out_specs=pl.BlockSpec((1, tq, D_ATTN), lambda b, i: (b, i, 0)),
        scratch_shapes=[pltpu.VMEM((N_KV, rows, 2 * LANES), BF16),
                        pltpu.VMEM((N_KV, rows, 2 * LANES), F32),
                        pltpu.VMEM((N_KV, rows, 2 * LANES), F32),
                        pltpu.VMEM((N_KV, rows, LANES), F32),
                        pltpu.VMEM((N_KV, rows, tk), BF16),
                        pltpu.VMEM((N_KV, rows, tk), BF16)],
        compiler_params=pltpu.CompilerParams(dimension_semantics=("arbitrary", "arbitrary"),
                                             vmem_limit_bytes=VMEM_LIMIT),
        name="nsa_prompt",
    )(qpad, gbr, ck, cvt, kaug0, kaug1, vsb, kwb, vwb, c2st)


def _nsa_sample_cmp_kernel(q_ref, ck_ref, cv_ref, c2s_ref, ocmp_ref, idx_ref, imp_ref, *, past, nb):
    nc = ck_ref.shape[1]
    nl = c2s_ref.shape[1]
    cend = lax.broadcasted_iota(jnp.int32, (1, nc), 1) * CMP_STRIDE + (CMP_LEN - 1)
    cmask = cend <= past
    row = lax.broadcasted_iota(jnp.int32, (8, 1), 0)

    def one_sequence(b, carry):
        q = q_ref[b]
        s = jnp.where(cmask, _dot_nt(q, ck_ref[b]), -BIG)
        m = jnp.max(s, axis=1, keepdims=True)
        e = jnp.exp2(s - m)
        p = e * jnp.where(m > -0.5 * BIG, 1.0 / jnp.sum(e, axis=1, keepdims=True), 0.0)
        ocmp_ref[b] = _dot(p.astype(BF16), cv_ref[b])
        ps0 = jnp.sum(p[0:Q_PER_KV], axis=0, keepdims=True)
        ps1 = jnp.sum(p[Q_PER_KV:2 * Q_PER_KV], axis=0, keepdims=True)
        psum = jnp.where(row == 0, ps0, jnp.where(row == 1, ps1, 0.0))
        hi, lo = _split_bf16(psum)
        imp_ref[pl.ds(pl.multiple_of(b * 8, 8), 8), :] = _dot(hi, c2s_ref[...]) + _dot(lo, c2s_ref[...])
        return carry

    lax.fori_loop(0, nb, one_sequence, 0)

    blk = lax.broadcasted_iota(jnp.int32, (nb * 8, nl), 1)
    out_lane = lax.broadcasted_iota(jnp.int32, (nb * 8, LANES), 1)
    picks = [jnp.full((nb * 8, LANES), -1, jnp.int32)]

    def on_pick(it, idx, chosen, valid):
        picks[0] = jnp.where(out_lane == it, jnp.where(valid, idx.astype(jnp.int32), -1), picks[0])

    _select_topk(_importance(imp_ref[...], blk, past // SEL_LEN), blk, 1, on_pick)
    idx_ref[...] = picks[0]


def _nsa_sample_cmp(q16, ck, cv, c2s, past):
    bsz = q16.shape[0]
    full = lambda a: pl.BlockSpec(a.shape, lambda i: (0,) * a.ndim)
    return pl.pallas_call(
        functools.partial(_nsa_sample_cmp_kernel, past=past, nb=bsz),
        out_shape=[jax.ShapeDtypeStruct((bsz, 16, LANES), F32), jax.ShapeDtypeStruct((bsz * 8, LANES), jnp.int32)],
        grid=(1,),
        in_specs=[full(q16), full(ck), full(cv), full(c2s)],
        out_specs=[pl.BlockSpec((bsz, 16, LANES), lambda i: (0, 0, 0)),
                   pl.BlockSpec((bsz * 8, LANES), lambda i: (0, 0))],
        scratch_shapes=[pltpu.VMEM((bsz * 8, c2s.shape[1]), F32)],
        compiler_params=pltpu.CompilerParams(dimension_semantics=("arbitrary",),
                                             vmem_limit_bytes=VMEM_LIMIT),
        name="nsa_sample_cmp",
    )(q16, ck, cv, c2s)


def _nsa_sample_attn_kernel(pt_ref, idx_ref, cache_ref, q_ref, ocmp_ref, gate_ref, win_ref, new_ref,
                            o_ref, buf_ref, sem_ref, *, past, nb):
    b = pl.program_id(0)
    nblk = N_KV * N_SELECT
    npast_blk = past // SEL_LEN
    per_page = PAGE // SEL_LEN
    slot = b % 2

    def blk_copy(bb, sl, i):
        j = jnp.clip(idx_ref[bb, i], 0, npast_blk - 1)
        return pltpu.make_async_copy(
            cache_ref.at[pt_ref[bb, j // per_page], pl.ds(2 * KV_W, 2 * KV_W), :],
            buf_ref.at[sl, i // N_SELECT, :, pl.ds(pl.multiple_of((i % N_SELECT) * PAGE, PAGE), PAGE)],
            sem_ref.at[sl])

    def start_all(bb, sl):
        def body(i, c):
            blk_copy(bb, sl, i).start()
            return c
        lax.fori_loop(0, nblk, body, 0)

    def wait_all(bb, sl):
        def body(i, c):
            blk_copy(bb, sl, i).wait()
            return c
        lax.fori_loop(0, nblk, body, 0)

    @pl.when(b == 0)
    def _():
        start_all(0, 0)

    @pl.when(b + 1 < nb)
    def _():
        start_all(b + 1, 1 - slot)

    q = q_ref[0]
    qf = q.astype(F32)
    knew = new_ref[0, 0:1, :]
    ks_new = knew[:, 0:128].astype(BF16).astype(F32)
    vs_new = knew[:, 128:256].astype(BF16).astype(F32)
    kw_new = knew[:, 256:384].astype(BF16).astype(F32)
    vw_new = knew[:, 384:512].astype(BF16).astype(F32)
    row = lax.broadcasted_iota(jnp.int32, (16, 1), 0)

    s = _dot(q, win_ref[0, 0:KV_W, :].astype(BF16))
    s_new = jnp.sum(qf * kw_new, axis=1, keepdims=True)
    m = jnp.maximum(jnp.max(s, axis=1, keepdims=True), s_new)
    e = jnp.exp2(s - m)
    e_new = jnp.exp2(s_new - m)
    o_win = ((_dot_nt(e.astype(BF16), win_ref[0, KV_W:2 * KV_W, :].astype(BF16))
              + e_new.astype(BF16).astype(F32) * vw_new) / (jnp.sum(e, axis=1, keepdims=True) + e_new))

    wait_all(b, slot)

    nkeys = N_SELECT * PAGE
    lane = lax.broadcasted_iota(jnp.int32, (1, nkeys), 1)
    s_new = jnp.sum(qf * ks_new, axis=1, keepdims=True)
    o_sel = jnp.zeros((16, LANES), F32)
    for g in range(N_KV):
        kpos = jnp.full((1, nkeys), past + 1, jnp.int32)
        for n in range(N_SELECT):
            j = idx_ref[b, g * N_SELECT + n]
            ok = (j >= 0) & (j < npast_blk)
            in_blk = (lane // PAGE == n) & ((lane % PAGE) // SEL_LEN == j % per_page) & ok
            kpos = jnp.where(in_blk, (j // per_page) * PAGE + lane % PAGE, kpos)
        s = jnp.where(kpos <= past, _dot(q, buf_ref[slot, g, 0:KV_W, :].astype(BF16)), -BIG)
        m = jnp.maximum(jnp.max(s, axis=1, keepdims=True), s_new)
        e = jnp.exp2(s - m)
        e_new = jnp.exp2(s_new - m)
        og = ((_dot_nt(e.astype(BF16), buf_ref[slot, g, KV_W:2 * KV_W, :].astype(BF16))
               + e_new.astype(BF16).astype(F32) * vs_new) / (jnp.sum(e, axis=1, keepdims=True) + e_new))
        o_sel = jnp.where((row >= g * Q_PER_KV) & (row < (g + 1) * Q_PER_KV), og, o_sel)

    gt = gate_ref[0]
    o_ref[0] = ocmp_ref[0] * gt[:, 0:1] + o_sel * gt[:, 1:2] + o_win * gt[:, 2:3]


def _nsa_sample_attn(page_ids, idx, cache_t, q16, ocmp, gate16, win_t, new_rows, layer, past):
    bsz = q16.shape[0]
    nwin = win_t.shape[3]
    return pl.pallas_call(
        functools.partial(_nsa_sample_attn_kernel, past=past, nb=bsz),
        out_shape=jax.ShapeDtypeStruct((bsz, 16, LANES), F32),
        grid_spec=pltpu.PrefetchScalarGridSpec(
            num_scalar_prefetch=2,
            grid=(bsz,),
            in_specs=[pl.BlockSpec(memory_space=pl.ANY),
                      pl.BlockSpec((1, 16, LANES), lambda b, pt, ix: (b, 0, 0)),
                      pl.BlockSpec((1, 16, LANES), lambda b, pt, ix: (b, 0, 0)),
                      pl.BlockSpec((1, 16, LANES), lambda b, pt, ix: (b, 0, 0)),
                      pl.BlockSpec((None, 1, 2 * KV_W, nwin), lambda b, pt, ix: (layer, b, 0, 0)),
                      pl.BlockSpec((1, 8, 512), lambda b, pt, ix: (b, 0, 0))],
            out_specs=pl.BlockSpec((1, 16, LANES), lambda b, pt, ix: (b, 0, 0)),
            scratch_shapes=[pltpu.VMEM((2, N_KV, 2 * KV_W, N_SELECT * PAGE), F32),
                            pltpu.SemaphoreType.DMA((2,))],
        ),
        compiler_params=pltpu.CompilerParams(dimension_semantics=("arbitrary",)),
        name="nsa_sample_attn",
    )(page_ids, idx, cache_t, q16, ocmp, gate16, win_t, new_rows)


def _pool_kernel(u_ref, halo_ref, pw_ref, ps_ref, y_ref, ext_ref, *, tm):
    i = pl.program_id(1)
    hb = POOL_BUF + 1
    ext_ref[0:hb, :] = jnp.where(i > 0, halo_ref[0], 0.0)
    ext_ref[hb:hb + tm, :] = u_ref[0]
    r = i * tm + lax.broadcasted_iota(jnp.int32, (tm, 1), 0)
    for g, w in enumerate(POOL_WINDOWS):
        cols = slice(g * POOL_GW, (g + 1) * POOL_GW)
        acc = ext_ref[hb:hb + tm, cols]
        for k in range(1, w):
            acc = acc + ext_ref[hb - k:hb - k + tm, cols]
        cnt = jnp.minimum(r + 1, w).astype(F32)
        d = acc / cnt - ext_ref[hb:hb + tm, cols]
        y_ref[0, :, cols] = _dot(d.astype(BF16), pw_ref[g]) * ps_ref[:, cols]


def _pool(u, pw, ps, tm):
    bsz, seq = u.shape[:2]
    hb = POOL_BUF + 1
    ratio = tm // hb
    return pl.pallas_call(
        functools.partial(_pool_kernel, tm=tm),
        out_shape=jax.ShapeDtypeStruct(u.shape, F32),
        grid=(bsz, seq // tm),
        in_specs=[pl.BlockSpec((1, tm, D_POOL), lambda b, i: (b, i, 0)),
                  pl.BlockSpec((1, hb, D_POOL), lambda b, i: (b, jnp.maximum(i * ratio - 1, 0), 0)),
                  pl.BlockSpec(pw.shape, lambda b, i: (0, 0, 0)),
                  pl.BlockSpec(ps.shape, lambda b, i: (0, 0))],
        out_specs=pl.BlockSpec((1, tm, D_POOL), lambda b, i: (b, i, 0)),
        scratch_shapes=[pltpu.VMEM((hb + tm, D_POOL), F32)],
        compiler_params=pltpu.CompilerParams(dimension_semantics=("arbitrary", "arbitrary")),
        name="pool",
    )(u, u, pw, ps)


def _mixer_out_kernel(h_ref, py_ref, gp_ref, at_ref, ga_ref, p_ref, wo_ref, wg_ref, wp_ref, o_ref):
    mp = (py_ref[...] * _silu(gp_ref[...])).astype(BF16)
    ma = (at_ref[...] * _silu(ga_ref[...])).astype(BF16)
    h1 = h_ref[...] + _dot(mp, wo_ref[0:D_POOL, :]) + _dot(ma, wo_ref[D_POOL:D_POOL + D_ATTN, :])
    gate = jax.nn.sigmoid(_dot(h1.astype(BF16), wg_ref[...]))
    o_ref[...] = h1 + gate * _dot(p_ref[...].astype(BF16), wp_ref[...])


def _mixer_out(h, py, gp, at, ga, p_all, layer, wo, wg, wp, tm):
    n = h.shape[0]
    row = lambda a: pl.BlockSpec((tm, a.shape[1]), lambda i: (i, 0))
    full = lambda a: pl.BlockSpec(a.shape, lambda i: (0, 0))
    p_spec = pl.BlockSpec((None, tm, p_all.shape[2]), lambda i: (layer, i, 0))
    return pl.pallas_call(
        _mixer_out_kernel,
        out_shape=jax.ShapeDtypeStruct(h.shape, F32),
        grid=(n // tm,),
        in_specs=[row(h), row(py), row(gp), row(at), row(ga), p_spec, full(wo), full(wg), full(wp)],
        out_specs=pl.BlockSpec((tm, D_MODEL), lambda i: (i, 0)),
        compiler_params=pltpu.CompilerParams(dimension_semantics=("arbitrary",),
                                             vmem_limit_bytes=VMEM_LIMIT),
        name="mixer_out",
    )(h, py, gp, at, ga, p_all, wo, wg, wp)


def _rope_tables(pos):
    half = ROPE_DIM // 2
    inv = ROPE_THETA ** (-jnp.arange(0, ROPE_DIM, 2, dtype=F32) / ROPE_DIM)
    ang = pos.astype(F32)[:, None] * inv[None, :]
    cos, sin = jnp.cos(ang), jnp.sin(ang)
    n = pos.shape[0]
    rest = HEAD_DIM - ROPE_DIM
    c = jnp.concatenate([cos, cos, jnp.ones((n, rest), F32)], axis=1)
    s1 = jnp.concatenate([jnp.zeros((n, half), F32), sin, jnp.zeros((n, rest), F32)], axis=1)
    s2 = jnp.concatenate([-sin, jnp.zeros((n, half + rest), F32)], axis=1)
    return tuple(jnp.tile(t, (1, LANES // HEAD_DIM)) for t in (c, s1, s2))


def _cmp_to_sel(nc, n_cmp, nl):
    ci = jnp.arange(nc)[:, None] * CMP_STRIDE
    sj = jnp.arange(nl)[None, :] * SEL_LEN
    m = (ci < sj + SEL_LEN) & (ci + CMP_LEN > sj) & (jnp.arange(nc)[:, None] < n_cmp)
    return m.astype(BF16)


def _layer_params(l, norm_g, w_in, q_norm, k_norm, cmp_pe, cmp_w1, cmp_w2, pool_w, pool_scale, w_out,
                  ple_gate, ple_proj):
    w = w_in[l]
    pad = jnp.zeros((D_MODEL, D_INR - 2840), w.dtype)
    w_r = jnp.concatenate([w[:, 0:2304], w[:, 2328:2840], w[:, 2304:2328], pad], axis=1).astype(BF16)
    ones = jnp.ones((KV_W,), F32)
    gain = jnp.concatenate([jnp.tile(q_norm[l], N_HEADS) * Q_SCALE,
                            jnp.tile(k_norm[l, 0], N_KV), jnp.tile(k_norm[l, 1], N_KV),
                            jnp.tile(k_norm[l, 2], N_KV), ones, ones, ones])[None, :].astype(F32)
    eye = jnp.eye(N_KV, dtype=F32)

    def taps(w1):
        w1r = w1.reshape(2, CMP_STRIDE, HEAD_DIM, HEAD_DIM)
        return jnp.einsum('gh,atde->tgdahe', eye, w1r).reshape(CMP_STRIDE, KV_W, 2 * KV_W).astype(BF16)

    w2 = jnp.stack([jnp.kron(eye, cmp_w2[l, i]) for i in range(2)]).astype(BF16)
    pe2 = jnp.concatenate([jnp.tile(cmp_pe[l, 0], (1, N_KV)), jnp.tile(cmp_pe[l, 1], (1, N_KV))], axis=1)
    return dict(ng=norm_g[l][None, :], w=w_r, gain=gain, wk=taps(cmp_w1[l, 0]), wv=taps(cmp_w1[l, 1]),
                w2=w2, pe2=pe2, pw=pool_w[l].astype(BF16), ps=pool_scale[l][None, :],
                wo=w_out[l].astype(BF16), wg=ple_gate[l].astype(BF16), wp=ple_proj[l].astype(BF16))


def kernel(x_prompt, x_sample, cache_kv, state_win, state_pool, page_table, p_prompt, p_sample,
           norm_g, w_in, q_norm, k_norm, cmp_pe, cmp_w1, cmp_w2, pool_w, pool_scale, w_out,
           ple_gate, ple_proj):
    depth = w_in.shape[0]
    bsz, seq, _ = x_prompt.shape
    dec_b, n_pages = page_table.shape
    n_pool, page_size = cache_kv.shape[1:3]
    past = n_pages * page_size
    n_win = state_win.shape[2]
    assert page_size == PAGE and seq % 512 == 0 and seq >= WINDOW + 128 and past >= WINDOW
    assert x_sample.shape[1] == 1 and past % SEL_LEN == 0 and n_win % LANES == 0
    n_sel_p = seq // SEL_LEN
    n_sel_s = past // SEL_LEN + 1
    assert N_SELECT <= n_sel_p <= LANES
    nl_s = -(-n_sel_s // LANES) * LANES
    nc_p, nc_s = seq // CMP_STRIDE, past // CMP_STRIDE

    tm, tm_out = 256, 512
    tabs_p = _rope_tables(jnp.arange(seq))
    tabs_s = _rope_tables(jnp.full((dec_b,), past, jnp.int32))
    jm = jnp.kron(jnp.eye(2 * LANES // HEAD_DIM, dtype=F32), jnp.full((HEAD_DIM, HEAD_DIM), 1.0 / HEAD_DIM, F32)).astype(BF16)
    c2st_p = _cmp_to_sel(nc_p, nc_p - 1, LANES).T
    c2s_s = _cmp_to_sel(nc_s, nc_s - 1, nl_s)
    ids_p = jnp.arange(bsz * seq // PAGE, dtype=jnp.int32).reshape(bsz, seq // PAGE)
    cache_t = jnp.transpose(cache_kv, (0, 1, 3, 4, 5, 2)).reshape(depth * n_pool, 4 * KV_W, PAGE)
    win_t = jnp.transpose(state_win, (0, 1, 3, 4, 5, 2)).reshape(depth, dec_b, 2 * KV_W, n_win)

    hp = x_prompt.reshape(bsz * seq, D_MODEL)
    hs = x_sample.reshape(dec_b, D_MODEL)
    outs = [[] for _ in range(6)]
    for l in range(depth):
        prm = _layer_params(l, norm_g, w_in, q_norm, k_norm, cmp_pe, cmp_w1, cmp_w2, pool_w, pool_scale,
                            w_out, ple_gate, ple_proj)
        u, gp, qpad, kv, win, ga, gbr, kaug0, kaug1, vsb, kwb, vwb = _mixer_in(
            hp, tabs_p, prm['ng'], prm['w'], prm['gain'], jm, tm)
        ck, _, cvt = _compress(ids_p, kv.reshape(bsz * seq // PAGE, PAGE, 512), prm['wk'], prm['wv'], prm['pe2'],
                               prm['w2'], False)
        r3 = lambda a: a.reshape(bsz, seq, a.shape[-1])
        attn = _nsa_prompt(r3(qpad), r3(gbr), ck, cvt, r3(kaug0), r3(kaug1), r3(vsb), r3(kwb), r3(vwb), c2st_p, 256, 512)
        py = _pool(r3(u), prm['pw'], prm['ps'], tm_out)
        hp = _mixer_out(hp, py.reshape(bsz * seq, D_POOL), gp, attn.reshape(bsz * seq, D_ATTN), ga,
                        p_prompt.reshape(depth, bsz * seq, D_PLE), l, prm['wo'], prm['wg'], prm['wp'], tm_out)
        outs[0].append(kv.reshape(bsz, seq, 4, N_KV, HEAD_DIM))
        outs[2].append(r3(win)[:, seq - min(WINDOW, seq):].reshape(bsz, min(WINDOW, seq), 2, N_KV, HEAD_DIM))
        outs[4].append(r3(u)[:, seq - POOL_BUF:])

        u, gp, qpad, kv, win, ga, gbr, _, _, _, _, _ = _mixer_in(
            hs, tabs_s, prm['ng'], prm['w'], prm['gain'], jm, dec_b)
        ids_s = page_table + l * n_pool
        ck, cv, _ = _compress(ids_s, cache_t, prm['wk'], prm['wv'], prm['pe2'], prm['w2'], True)
        q16 = jnp.pad(qpad.reshape(dec_b, N_HEADS, LANES), ((0, 0), (0, 16 - N_HEADS), (0, 0)))
        ocmp, idx = _nsa_sample_cmp(q16, ck, cv, c2s_s, past)
        gate16 = jnp.pad(gbr[:, :3 * N_HEADS].reshape(dec_b, N_HEADS, 3), ((0, 0), (0, 16 - N_HEADS), (0, LANES - 3)))
        new_rows = jnp.broadcast_to(jnp.concatenate([kv[:, 256:512], win], axis=1)[:, None, :], (dec_b, 8, 512))
        idx = idx.reshape(dec_b, 8, LANES)[:, :N_KV, :N_SELECT].reshape(dec_b, N_KV * N_SELECT)
        o16 = _nsa_sample_attn(ids_s, idx, cache_t, q16, ocmp,
                               gate16, win_t, new_rows, l, past)
        attn = jnp.concatenate([o16[:, h, (h // Q_PER_KV) * HEAD_DIM:(h // Q_PER_KV + 1) * HEAD_DIM]
                                for h in range(N_HEADS)], axis=1)
        u_ext = jnp.concatenate([state_pool[l], u[:, None, :]], axis=1)
        py = _pool(u_ext, prm['pw'], prm['ps'], POOL_BUF + 1)[:, POOL_BUF]
        hs = _mixer_out(hs, py, gp, attn, ga, p_sample.reshape(depth, dec_b, D_PLE), l, prm['wo'], prm['wg'], prm['wp'], dec_b)
        outs[1].append(kv.reshape(dec_b, 1, 4, N_KV, HEAD_DIM))
        outs[3].append(win.reshape(dec_b, 1, 2, N_KV, HEAD_DIM))
        outs[5].append(u_ext[:, 1:])

    return (hp.reshape(bsz, seq, D_MODEL), hs.reshape(dec_b, 1, D_MODEL),
            jnp.stack(outs[0]), jnp.stack(outs[1]), jnp.stack(outs[2]),
            jnp.concatenate([state_win[:, :, 1:], jnp.stack(outs[3])], axis=2),
            jnp.stack(outs[4]), jnp.stack(outs[5]))
```

```python
import functools
import math

import jax
import jax.numpy as jnp
from jax import lax
from jax.experimental import pallas as pl
from jax.experimental.pallas import tpu as pltpu

F32 = jnp.float32
BF16 = jnp.bfloat16

D_MODEL = 1024
D_POOL = 512
POOL_WINDOWS = (2, 4, 8, 16)
POOL_GW = 128
POOL_BUF = 15
D_ATTN = 512
HEAD_DIM = 64
N_HEADS = 8
N_KV = 2
Q_PER_KV = 4
KV_W = 128
ROPE_DIM = 16
ROPE_THETA = 500000.0
CMP_LEN = 32
CMP_STRIDE = 16
SEL_LEN = 64
N_SELECT = 16
WINDOW = 512
D_PLE = 256
EPS = 1e-6
BIG = 1e30
NEG = -3.0e38
LANES = 128
D_INR = 2944
PAGE = 128
VMEM_LIMIT = 56 * 1024 * 1024
Q_SCALE = HEAD_DIM ** -0.5 * math.log2(math.e)
SHIFT_SLACK = 60.0


def _dot(a, b):
    return jnp.dot(a, b, preferred_element_type=F32)


def _dot_nt(a, b):
    return lax.dot_general(a, b, (((1,), (1,)), ((), ())), preferred_element_type=F32)


def _split_bf16(x):
    hi = x.astype(BF16)
    return hi, (x - hi.astype(F32)).astype(BF16)


def _silu(x):
    return x * jax.nn.sigmoid(x)


def _mixer_in_kernel(x_ref, cos_ref, s1_ref, s2_ref, ng_ref, w_ref, gain_ref, j_ref,
                     u_ref, gp_ref, q_ref, kv_ref, win_ref, ga_ref, gbr_ref,
                     kaug0_ref, kaug1_ref, vsb_ref, kwb_ref, vwb_ref, *, tm, nper):
    x = x_ref[...]
    ms = jnp.mean(x * x, axis=-1, keepdims=True)
    xn = (x * lax.rsqrt(ms + EPS) * ng_ref[...]).astype(BF16)
    cos, s1, s2 = cos_ref[...], s1_ref[...], s2_ref[...]
    lane = lax.broadcasted_iota(jnp.int32, (tm, LANES), 1)
    lo_half = lane < HEAD_DIM

    def mean_sq(chunks):
        wd = LANES * len(chunks)
        sq = jnp.concatenate([c * c for c in chunks], axis=1).astype(BF16)
        msq = _dot(sq, j_ref[0:wd, 0:wd])
        return [msq[:, i * LANES:(i + 1) * LANES] for i in range(len(chunks))]

    def normrope(zc, msq, gain):
        y = zc * lax.rsqrt(msq + EPS) * gain
        return y * cos + pltpu.roll(y, 8, 1) * s1 + pltpu.roll(y, LANES - 8, 1) * s2

    u_ref[...] = _dot(xn, w_ref[:, 0:512])
    gp_ref[...] = _dot(xn, w_ref[:, 512:1024])
    zq = _dot(xn, w_ref[:, 1024:1536])
    zkv = _dot(xn, w_ref[:, 1536:2304])
    zqc = [zq[:, c * LANES:(c + 1) * LANES] for c in range(4)]
    mq = mean_sq(zqc[0:2]) + mean_sq(zqc[2:4])
    mkc, mks = mean_sq([zkv[:, 0:128], zkv[:, 256:384]])
    mkw, = mean_sq([zkv[:, 512:640]])
    for c in range(4):
        yc = normrope(zqc[c], mq[c], gain_ref[:, c * LANES:(c + 1) * LANES])
        yr = pltpu.roll(yc, HEAD_DIM, 1)
        if c // 2 == 0:
            h0 = jnp.where(lo_half, yc, 0.0)
            h1 = jnp.where(lo_half, yr, 0.0)
        else:
            h0 = jnp.where(lo_half, 0.0, yr)
            h1 = jnp.where(lo_half, 0.0, yc)
        q_ref[:, (2 * c) * LANES:(2 * c + 1) * LANES] = h0.astype(BF16)
        q_ref[:, (2 * c + 1) * LANES:(2 * c + 2) * LANES] = h1.astype(BF16)
    kc = normrope(zkv[:, 0:128], mkc, gain_ref[:, 512:640])
    vc = zkv[:, 128:256]
    ks = normrope(zkv[:, 256:384], mks, gain_ref[:, 640:768])
    vs = zkv[:, 384:512]
    kw = normrope(zkv[:, 512:640], mkw, gain_ref[:, 768:896])
    vw = zkv[:, 640:768]
    kv_ref[:, 0:128] = kc
    kv_ref[:, 128:256] = vc
    kv_ref[:, 256:384] = ks
    kv_ref[:, 384:512] = vs
    win_ref[:, 0:128] = kw
    win_ref[:, 128:256] = vw
    ga_ref[...] = _dot(xn, w_ref[:, 2304:2816])
    gbr_ref[...] = jax.nn.sigmoid(_dot(xn, w_ref[:, 2816:2944]))
    pos = (pl.program_id(0) % nper) * tm + lax.broadcasted_iota(jnp.int32, (tm, LANES), 0)
    ones = jnp.ones((tm, LANES), BF16)
    onehot = jnp.where(lane == pos // SEL_LEN, 1.0, 0.0).astype(BF16)
    kaug0_ref[:, 0:128] = jnp.where(lo_half, ks, jnp.where(lane == HEAD_DIM, 1.0, 0.0)).astype(BF16)
    kaug0_ref[:, 128:256] = onehot
    kaug1_ref[:, 0:128] = jnp.where(lo_half, jnp.where(lane == 0, 1.0, 0.0), ks).astype(BF16)
    kaug1_ref[:, 128:256] = onehot
    vsb_ref[:, 0:128] = vs.astype(BF16)
    vsb_ref[:, 128:256] = ones
    kwb_ref[...] = kw.astype(BF16)
    vwb_ref[:, 0:128] = vw.astype(BF16)
    vwb_ref[:, 128:256] = ones


def _mixer_in(x, tabs, ng, w, gain, jm, tm):
    n = x.shape[0]
    nper = tabs[0].shape[0] // tm
    row = lambda width: pl.BlockSpec((tm, width), lambda i: (i, 0))
    full = lambda a: pl.BlockSpec(a.shape, lambda i: (0,) * a.ndim)
    tab = pl.BlockSpec((tm, LANES), lambda i: (i % nper, 0))
    widths = [(512, F32), (512, F32), (1024, BF16), (512, F32), (256, F32), (512, F32), (128, F32),
              (256, BF16), (256, BF16), (256, BF16), (128, BF16), (256, BF16)]
    return pl.pallas_call(
        functools.partial(_mixer_in_kernel, tm=tm, nper=nper),
        out_shape=[jax.ShapeDtypeStruct((n, wd), dt) for wd, dt in widths],
        grid=(n // tm,),
        in_specs=[row(D_MODEL), tab, tab, tab, full(ng), full(w), full(gain), full(jm)],
        out_specs=[row(wd) for wd, _ in widths],
        compiler_params=pltpu.CompilerParams(dimension_semantics=("arbitrary",),
                                             vmem_limit_bytes=VMEM_LIMIT),
        name="mixer_in",
    )(x, *tabs, ng, w, gain, jm)


def _compress_kernel(pt_ref, src_ref, wk_ref, wv_ref, pe_ref, w2_ref, perm_ref, ck_ref, cv_ref, cvt_ref,
                     *scratch, n_pages, nb, feature_major):
    if feature_major:
        stage_ref, tapk_ref, tapv_ref, bias_ref, sem_ref = scratch
    else:
        buf_ref, bias_ref, sem_ref = scratch
    b = pl.program_id(0)
    nchunk = n_pages * PAGE // CMP_STRIDE
    slot = b % 2

    def page_copies(bb, sl, j):
        if feature_major:
            return [pltpu.make_async_copy(src_ref.at[pt_ref[bb, j], pl.ds(0, 2 * KV_W), :],
                                          stage_ref.at[sl, j], sem_ref.at[sl])]
        return [pltpu.make_async_copy(src_ref.at[pt_ref[bb, j], :, pl.ds(i * KV_W, KV_W)],
                                      buf_ref.at[sl, i, pl.ds(j * PAGE, PAGE), :], sem_ref.at[sl])
                for i in range(2)]

    def start_all(bb, sl):
        def body(j, c):
            for cp in page_copies(bb, sl, j):
                cp.start()
            return c
        lax.fori_loop(0, n_pages, body, 0)

    def wait_all(bb, sl):
        def body(j, c):
            for cp in page_copies(bb, sl, j):
                cp.wait()
            return c
        lax.fori_loop(0, n_pages, body, 0)

    @pl.when(b == 0)
    def _():
        start_all(0, 0)
        acc = jnp.zeros((8, 256), F32)
        for t in range(CMP_STRIDE):
            a0 = jnp.broadcast_to(pe_ref[t:t + 1, :], (8, 256)).astype(BF16)
            a1 = jnp.broadcast_to(pe_ref[CMP_STRIDE + t:CMP_STRIDE + t + 1, :], (8, 256)).astype(BF16)
            pk0 = _dot(a0[:, 0:128], wk_ref[t])
            pk1 = _dot(a1[:, 0:128], wk_ref[t])
            pv0 = _dot(a0[:, 128:256], wv_ref[t])
            pv1 = _dot(a1[:, 128:256], wv_ref[t])
            acc = acc + jnp.concatenate([pk0[:, 0:128] + pk1[:, 128:256],
                                         pv0[:, 0:128] + pv1[:, 128:256]], axis=1)
        bias_ref[...] = acc

    @pl.when(b + 1 < nb)
    def _():
        start_all(b + 1, 1 - slot)

    wait_all(b, slot)

    acc_k = jnp.zeros((nchunk, 256), F32)
    acc_v = jnp.zeros((nchunk, 256), F32)
    if feature_major:
        def regroup(jj, c):
            x2 = jnp.concatenate([stage_ref[slot, 2 * jj], stage_ref[slot, 2 * jj + 1]], axis=1).astype(BF16)
            r0 = pl.multiple_of(jj * CMP_STRIDE, CMP_STRIDE)
            z = _dot_nt(perm_ref[...], x2).astype(BF16)
            for i, tap_ref in enumerate((tapk_ref, tapv_ref)):
                for t in range(CMP_STRIDE):
                    tap_ref[t // 2, pl.ds(r0, CMP_STRIDE), (t % 2) * KV_W:(t % 2 + 1) * KV_W] = (
                        z[t * CMP_STRIDE:(t + 1) * CMP_STRIDE, i * KV_W:(i + 1) * KV_W])
            return c
        lax.fori_loop(0, n_pages // 2, regroup, 0, unroll=8)
        for u in range(CMP_STRIDE // 2):
            acc_k = acc_k + _dot(tapk_ref[u], jnp.concatenate([wk_ref[2 * u], wk_ref[2 * u + 1]], axis=0))
            acc_v = acc_v + _dot(tapv_ref[u], jnp.concatenate([wv_ref[2 * u], wv_ref[2 * u + 1]], axis=0))
    else:
        for t in range(CMP_STRIDE):
            xk = buf_ref[slot, 0, pl.ds(t, nchunk, stride=CMP_STRIDE), :]
            xv = buf_ref[slot, 1, pl.ds(t, nchunk, stride=CMP_STRIDE), :]
            acc_k = acc_k + _dot(xk.astype(BF16), wk_ref[t])
            acc_v = acc_v + _dot(xv.astype(BF16), wv_ref[t])
    row = lax.broadcasted_iota(jnp.int32, (nchunk, LANES), 0)
    valid = row < nchunk - 1
    res = []
    for i, acc in enumerate((acc_k, acc_v)):
        pre = acc[:, 0:128] + pltpu.roll(acc[:, 128:256], nchunk - 1, 0) + bias_ref[0:1, i * 128:(i + 1) * 128]
        hid = _silu(pre).astype(BF16)
        res.append(jnp.where(valid, _dot(hid, w2_ref[i]), 0.0))
    ck_ref[0] = res[0].astype(BF16)
    cv_ref[0] = res[1].astype(BF16)
    cvt_ref[0] = res[1].T.astype(BF16)


def _compress(page_ids, src, wk, wv, pe2, w2, feature_major):
    nb, n_pages = page_ids.shape
    nchunk = n_pages * PAGE // CMP_STRIDE
    full = lambda a: pl.BlockSpec(a.shape, lambda b, pt: (0,) * a.ndim)
    r = jnp.arange(2 * PAGE)
    perm = (r[None, :] == (r[:, None] % CMP_STRIDE) * CMP_STRIDE + r[:, None] // CMP_STRIDE).astype(BF16)
    if feature_major:
        assert n_pages % 2 == 0
        scratch = [pltpu.VMEM((2, n_pages, 2 * KV_W, PAGE), F32),
                   pltpu.VMEM((CMP_STRIDE // 2, nchunk, 2 * KV_W), BF16),
                   pltpu.VMEM((CMP_STRIDE // 2, nchunk, 2 * KV_W), BF16)]
    else:
        scratch = [pltpu.VMEM((2, 2, n_pages * PAGE, KV_W), F32)]
    return pl.pallas_call(
        functools.partial(_compress_kernel, n_pages=n_pages, nb=nb, feature_major=feature_major),
        out_shape=[jax.ShapeDtypeStruct((nb, nchunk, KV_W), BF16), jax.ShapeDtypeStruct((nb, nchunk, KV_W), BF16),
                   jax.ShapeDtypeStruct((nb, KV_W, nchunk), BF16)],
        grid_spec=pltpu.PrefetchScalarGridSpec(
            num_scalar_prefetch=1,
            grid=(nb,),
            in_specs=[pl.BlockSpec(memory_space=pl.ANY), full(wk), full(wv), full(pe2), full(w2), full(perm)],
            out_specs=[pl.BlockSpec((1, nchunk, KV_W), lambda b, pt: (b, 0, 0)),
                       pl.BlockSpec((1, nchunk, KV_W), lambda b, pt: (b, 0, 0)),
                       pl.BlockSpec((1, KV_W, nchunk), lambda b, pt: (b, 0, 0))],
            scratch_shapes=scratch + [pltpu.VMEM((8, 256), F32), pltpu.SemaphoreType.DMA((2,))],
        ),
        compiler_params=pltpu.CompilerParams(dimension_semantics=("arbitrary",),
                                             vmem_limit_bytes=VMEM_LIMIT),
        name="compress",
    )(page_ids, src, wk, wv, pe2, w2, perm)


N_FORCED = 3


def _select_topk(x, blk, axis, on_pick, rounds=N_SELECT):
    nl = float(x.shape[axis])
    blkf = blk.astype(F32)
    for it in range(rounds):
        m = jnp.max(x, axis=axis, keepdims=True)
        idx = jnp.min(jnp.where(x == m, blkf, nl), axis=axis, keepdims=True)
        chosen = blkf == idx
        on_pick(it, idx, chosen, m > -0.5 * BIG)
        x = jnp.where(chosen, NEG, x)


def _importance(imp, blk, cur):
    d = cur - blk
    return jnp.where(blk == 0, BIG, jnp.where(d == 0, BIG, jnp.where(d == 1, BIG, jnp.where(d > 1, imp, -BIG))))


def _nsa_prompt_kernel(q_ref, gbr_ref, ck_ref, cvt_ref, kaug0_ref, kaug1_ref, vs_ref, kw_ref, vw_ref, c2st_ref,
                       o_ref, lhs_ref, acc_ref, bak_ref, c_ref, p0_ref, p1_ref, *, tq, tk):
    t0 = pl.program_id(1) * tq
    rows = Q_PER_KV * tq
    nc = ck_ref.shape[1]
    kaug_refs = (kaug0_ref, kaug1_ref)
    lane_r = lax.broadcasted_iota(jnp.int32, (rows, LANES), 1)
    qpos_r = t0 + lax.broadcasted_iota(jnp.int32, (rows, 1), 0) % tq
    qpos_c = t0 + lax.broadcasted_iota(jnp.int32, (1, rows), 1) % tq
    qpos_t = t0 + lax.broadcasted_iota(jnp.int32, (1, tq), 1)
    blk_t = lax.broadcasted_iota(jnp.int32, (LANES, tq), 0)
    cend = lax.broadcasted_iota(jnp.int32, (nc, 1), 0) * CMP_STRIDE + (CMP_LEN - 1)
    lo_half = lax.broadcasted_iota(jnp.int32, (tq, LANES), 1) < HEAD_DIM
    nwin = WINDOW + tq
    kstart = pl.multiple_of(jnp.maximum(t0 - WINDOW, 0), tq)
    cmask = cend <= qpos_c

    o_cmp, o_win = [], []
    for g in range(N_KV):
        for r in range(Q_PER_KV):
            h = Q_PER_KV * g + r
            lhs_ref[g, r * tq:(r + 1) * tq, 0:LANES] = q_ref[0, :, h * LANES:(h + 1) * LANES]
        qg = lhs_ref[g, :, 0:LANES]

        s = jnp.where(cmask, _dot_nt(ck_ref[0], qg), -BIG)
        m = jnp.max(s, axis=0, keepdims=True)
        e = jnp.exp2(s - m)
        p = e * jnp.where(m > -0.5 * BIG, 1.0 / jnp.sum(e, axis=0, keepdims=True), 0.0)
        o_cmp.append(_dot(cvt_ref[0], p.astype(BF16)).T)
        psum = p[:, 0:tq] + p[:, tq:2 * tq] + p[:, 2 * tq:3 * tq] + p[:, 3 * tq:4 * tq]
        hi, lo = _split_bf16(psum)
        imp = _dot(c2st_ref[...], hi) + _dot(c2st_ref[...], lo)

        d = qpos_t // SEL_LEN - blk_t
        forced = lambda yes, no: jnp.where(blk_t == 0, yes, jnp.where(d == 0, yes, jnp.where(d == 1, yes, no)))
        bias = [forced(0.0, jnp.full((LANES, tq), -BIG, F32))]

        def on_pick(it, idx, chosen, valid):
            bias[0] = jnp.where(chosen, jnp.where(valid, 0.0, -BIG), bias[0])

        _select_topk(forced(NEG, jnp.where(d > 1, imp, -BIG)), blk_t, 0, on_pick, N_SELECT - N_FORCED)
        selbias = bias[0].T.astype(BF16)
        for r in range(Q_PER_KV):
            lhs_ref[g, r * tq:(r + 1) * tq, LANES:2 * LANES] = selbias

        kwpos = kstart + lax.broadcasted_iota(jnp.int32, (1, nwin), 1)
        wmask = (kwpos <= qpos_r) & (kwpos >= qpos_r - WINDOW)
        s = jnp.where(wmask, _dot_nt(qg, kw_ref[0, pl.ds(kstart, nwin), :]), -BIG)
        e = jnp.exp2(s - jnp.max(s, axis=1, keepdims=True))
        ow = _dot(e.astype(BF16), vw_ref[0, pl.ds(kstart, nwin), :])
        o_win.append(ow[:, 0:LANES] / ow[:, LANES:2 * LANES])

    def key_rows(kt):
        return pl.ds(pl.multiple_of(kt * tk, tk), tk)

    def scores(g, kt):
        causal = (kt * tk + lax.broadcasted_iota(jnp.int32, (1, tk), 1)) <= qpos_r
        return jnp.where(causal, _dot_nt(lhs_ref[g], kaug_refs[g][0, key_rows(kt), :]), -BIG)

    def add_pv(g, p, kt):
        acc_ref[g] = acc_ref[g] + _dot(p, vs_ref[0, key_rows(kt), :])

    def recentre(g, kt, p_out, first):
        s_ = scores(g, kt)
        if not first:
            c_old = c_ref[g][:, 0:1]
            s_ = s_ + c_old
        mr = jnp.max(s_, axis=1, keepdims=True)
        c_new = (mr if first else jnp.maximum(c_old, mr)).astype(BF16).astype(F32)
        p_out[g] = jnp.exp2(s_ - c_new).astype(BF16)
        acc_ref[g] = jnp.zeros((rows, 2 * LANES), F32) if first else acc_ref[g] * jnp.exp2(c_old - c_new)
        c_ref[g] = jnp.broadcast_to(c_new, (rows, LANES))
        qf = lhs_ref[g, :, 0:LANES].astype(F32)
        lhs_ref[g, :, 0:LANES] = jnp.where(lane_r == HEAD_DIM * (1 - g), -c_new, qf).astype(BF16)

    for g in range(N_KV):
        recentre(g, 0, p0_ref, True)

    def fast_tile(g, kt, p_in, p_out):
        add_pv(g, p_in[g], kt - 1)
        s_ = scores(g, kt)
        p_out[g] = jnp.exp2(s_).astype(BF16)
        return jnp.max(s_)

    def sel_pair(j, carry):
        a = 2 * j + 1
        for g in range(N_KV):
            bak_ref[g] = acc_ref[g]
        top = [fast_tile(g, a, p0_ref, p1_ref) for g in range(N_KV)]
        top += [fast_tile(g, a + 1, p1_ref, p0_ref) for g in range(N_KV)]

        @pl.when(jnp.logical_not(jnp.maximum(jnp.maximum(top[0], top[1]), jnp.maximum(top[2], top[3])) <= SHIFT_SLACK))
        def _():
            for g in range(N_KV):
                acc_ref[g] = bak_ref[g]
                add_pv(g, jnp.exp2(scores(g, a - 1)).astype(BF16), a - 1)
                recentre(g, a, p1_ref, False)
                add_pv(g, p1_ref[g], a)
                recentre(g, a + 1, p0_ref, False)

        return carry

    n_tiles = (t0 + tq + tk - 1) // tk
    n_pairs = (n_tiles - 1) // 2
    lax.fori_loop(0, n_pairs, sel_pair, 0)
    odd_tail = n_tiles - 1 - 2 * n_pairs == 1

    @pl.when(odd_tail)
    def _():
        last = n_tiles - 1
        top = [fast_tile(g, last, p0_ref, p1_ref) for g in range(N_KV)]

        @pl.when(jnp.logical_not(jnp.maximum(top[0], top[1]) <= SHIFT_SLACK))
        def _():
            for g in range(N_KV):
                recentre(g, last, p1_ref, False)

        for g in range(N_KV):
            add_pv(g, p1_ref[g], last)

    @pl.when(jnp.logical_not(odd_tail))
    def _():
        for g in range(N_KV):
            add_pv(g, p0_ref[g], n_tiles - 1)


    for g in range(N_KV):
        acc = acc_ref[g]
        o_sel = acc[:, 0:LANES] / acc[:, LANES:2 * LANES]
        pieces = []
        for r in range(Q_PER_KV):
            h = Q_PER_KV * g + r
            sl = slice(r * tq, (r + 1) * tq)
            oh = (o_cmp[g][sl] * gbr_ref[0, :, 3 * h:3 * h + 1] + o_sel[sl] * gbr_ref[0, :, 3 * h + 1:3 * h + 2]
                  + o_win[g][sl] * gbr_ref[0, :, 3 * h + 2:3 * h + 3])
            pieces.append(oh if (h % 2) == g else pltpu.roll(oh, HEAD_DIM, 1))
        for c in range(2):
            o_ref[0, :, (2 * g + c) * LANES:(2 * g + c + 1) * LANES] = jnp.where(lo_half, pieces[2 * c], pieces[2 * c + 1])


def _nsa_prompt(qpad, gbr, ck, cvt, kaug0, kaug1, vsb, kwb, vwb, c2st, tq, tk):
    bsz, seq = qpad.shape[:2]
    rows = Q_PER_KV * tq
    per_b = lambda a: pl.BlockSpec((1,) + a.shape[1:], lambda b, i: (b, 0, 0), pipeline_mode=pl.Buffered(1))
    return pl.pallas_call(
        functools.partial(_nsa_prompt_kernel, tq=tq, tk=tk),
        out_shape=jax.ShapeDtypeStruct((bsz, seq, D_ATTN), F32),
        grid=(bsz, seq // tq),
        in_specs=[pl.BlockSpec((1, tq, N_HEADS * LANES), lambda b, i: (b, i, 0)),
                  pl.BlockSpec((1, tq, LANES), lambda b, i: (b, i, 0)),
                  per_b(ck), per_b(cvt), per_b(kaug0), per_b(kaug1), per_b(vsb), per_b(kwb), per_b(vwb),
                  pl.BlockSpec(c2st.shape, lambda b, i: (0, 0))],
        out_specs=pl.BlockSpec((1, tq, D_ATTN), lambda b, i: (b, i, 0)),
        scratch_shapes=[pltpu.VMEM((N_KV, rows, 2 * LANES), BF16),
                        pltpu.VMEM((N_KV, rows, 2 * LANES), F32),
                        pltpu.VMEM((N_KV, rows, 2 * LANES), F32),
                        pltpu.VMEM((N_KV, rows, LANES), F32),
                        pltpu.VMEM((N_KV, rows, tk), BF16),
                        pltpu.VMEM((N_KV, rows, tk), BF16)],
        compiler_params=pltpu.CompilerParams(dimension_semantics=("arbitrary", "arbitrary"),
                                             vmem_limit_bytes=VMEM_LIMIT),
        name="nsa_prompt",
    )(qpad, gbr, ck, cvt, kaug0, kaug1, vsb, kwb, vwb, c2st)


def _nsa_sample_cmp_kernel(q_ref, ck_ref, cv_ref, c2s_ref, ocmp_ref, idx_ref, imp_ref, *, past, nb):
    nc = ck_ref.shape[1]
    nl = c2s_ref.shape[1]
    cend = lax.broadcasted_iota(jnp.int32, (1, nc), 1) * CMP_STRIDE + (CMP_LEN - 1)
    cmask = cend <= past
    row = lax.broadcasted_iota(jnp.int32, (8, 1), 0)

    def one_sequence(b, carry):
        q = q_ref[b]
        s = jnp.where(cmask, _dot_nt(q, ck_ref[b]), -BIG)
        m = jnp.max(s, axis=1, keepdims=True)
        e = jnp.exp2(s - m)
        p = e * jnp.where(m > -0.5 * BIG, 1.0 / jnp.sum(e, axis=1, keepdims=True), 0.0)
        ocmp_ref[b] = _dot(p.astype(BF16), cv_ref[b])
        ps0 = jnp.sum(p[0:Q_PER_KV], axis=0, keepdims=True)
        ps1 = jnp.sum(p[Q_PER_KV:2 * Q_PER_KV], axis=0, keepdims=True)
        psum = jnp.where(row == 0, ps0, jnp.where(row == 1, ps1, 0.0))
        hi, lo = _split_bf16(psum)
        imp_ref[pl.ds(pl.multiple_of(b * 8, 8), 8), :] = _dot(hi, c2s_ref[...]) + _dot(lo, c2s_ref[...])
        return carry

    lax.fori_loop(0, nb, one_sequence, 0)

    blk = lax.broadcasted_iota(jnp.int32, (nb * 8, nl), 1)
    out_lane = lax.broadcasted_iota(jnp.int32, (nb * 8, LANES), 1)
    picks = [jnp.full((nb * 8, LANES), -1, jnp.int32)]

    def on_pick(it, idx, chosen, valid):
        picks[0] = jnp.where(out_lane == it, jnp.where(valid, idx.astype(jnp.int32), -1), picks[0])

    _select_topk(_importance(imp_ref[...], blk, past // SEL_LEN), blk, 1, on_pick)
    idx_ref[...] = picks[0]


def _nsa_sample_cmp(q16, ck, cv, c2s, past):
    bsz = q16.shape[0]
    full = lambda a: pl.BlockSpec(a.shape, lambda i: (0,) * a.ndim)
    return pl.pallas_call(
        functools.partial(_nsa_sample_cmp_kernel, past=past, nb=bsz),
        out_shape=[jax.ShapeDtypeStruct((bsz, 16, LANES), F32), jax.ShapeDtypeStruct((bsz * 8, LANES), jnp.int32)],
        grid=(1,),
        in_specs=[full(q16), full(ck), full(cv), full(c2s)],
        out_specs=[pl.BlockSpec((bsz, 16, LANES), lambda i: (0, 0, 0)),
                   pl.BlockSpec((bsz * 8, LANES), lambda i: (0, 0))],
        scratch_shapes=[pltpu.VMEM((bsz * 8, c2s.shape[1]), F32)],
        compiler_params=pltpu.CompilerParams(dimension_semantics=("arbitrary",),
                                             vmem_limit_bytes=VMEM_LIMIT),
        name="nsa_sample_cmp",
    )(q16, ck, cv, c2s)


def _nsa_sample_attn_kernel(pt_ref, idx_ref, cache_ref, q_ref, ocmp_ref, gate_ref, win_ref, new_ref,
                            o_ref, buf_ref, sem_ref, *, past, nb):
    b = pl.program_id(0)
    nblk = N_KV * N_SELECT
    npast_blk = past // SEL_LEN
    per_page = PAGE // SEL_LEN
    slot = b % 2

    def blk_copy(bb, sl, i):
        j = jnp.clip(idx_ref[bb, i], 0, npast_blk - 1)
        return pltpu.make_async_copy(
            cache_ref.at[pt_ref[bb, j // per_page], pl.ds(2 * KV_W, 2 * KV_W), :],
            buf_ref.at[sl, i // N_SELECT, :, pl.ds(pl.multiple_of((i % N_SELECT) * PAGE, PAGE), PAGE)],
            sem_ref.at[sl])

    def start_all(bb, sl):
        def body(i, c):
            blk_copy(bb, sl, i).start()
            return c
        lax.fori_loop(0, nblk, body, 0)

    def wait_all(bb, sl):
        def body(i, c):
            blk_copy(bb, sl, i).wait()
            return c
        lax.fori_loop(0, nblk, body, 0)

    @pl.when(b == 0)
    def _():
        start_all(0, 0)

    @pl.when(b + 1 < nb)
    def _():
        start_all(b + 1, 1 - slot)

    q = q_ref[0]
    qf = q.astype(F32)
    knew = new_ref[0, 0:1, :]
    ks_new = knew[:, 0:128].astype(BF16).astype(F32)
    vs_new = knew[:, 128:256].astype(BF16).astype(F32)
    kw_new = knew[:, 256:384].astype(BF16).astype(F32)
    vw_new = knew[:, 384:512].astype(BF16).astype(F32)
    row = lax.broadcasted_iota(jnp.int32, (16, 1), 0)

    s = _dot(q, win_ref[0, 0:KV_W, :].astype(BF16))
    s_new = jnp.sum(qf * kw_new, axis=1, keepdims=True)
    m = jnp.maximum(jnp.max(s, axis=1, keepdims=True), s_new)
    e = jnp.exp2(s - m)
    e_new = jnp.exp2(s_new - m)
    o_win = ((_dot_nt(e.astype(BF16), win_ref[0, KV_W:2 * KV_W, :].astype(BF16))
              + e_new.astype(BF16).astype(F32) * vw_new) / (jnp.sum(e, axis=1, keepdims=True) + e_new))

    wait_all(b, slot)

    nkeys = N_SELECT * PAGE
    lane = lax.broadcasted_iota(jnp.int32, (1, nkeys), 1)
    s_new = jnp.sum(qf * ks_new, axis=1, keepdims=True)
    o_sel = jnp.zeros((16, LANES), F32)
    for g in range(N_KV):
        kpos = jnp.full((1, nkeys), past + 1, jnp.int32)
        for n in range(N_SELECT):
            j = idx_ref[b, g * N_SELECT + n]
            ok = (j >= 0) & (j < npast_blk)
            in_blk = (lane // PAGE == n) & ((lane % PAGE) // SEL_LEN == j % per_page) & ok
            kpos = jnp.where(in_blk, (j // per_page) * PAGE + lane % PAGE, kpos)
        s = jnp.where(kpos <= past, _dot(q, buf_ref[slot, g, 0:KV_W, :].astype(BF16)), -BIG)
        m = jnp.maximum(jnp.max(s, axis=1, keepdims=True), s_new)
        e = jnp.exp2(s - m)
        e_new = jnp.exp2(s_new - m)
        og = ((_dot_nt(e.astype(BF16), buf_ref[slot, g, KV_W:2 * KV_W, :].astype(BF16))
               + e_new.astype(BF16).astype(F32) * vs_new) / (jnp.sum(e, axis=1, keepdims=True) + e_new))
        o_sel = jnp.where((row >= g * Q_PER_KV) & (row < (g + 1) * Q_PER_KV), og, o_sel)

    gt = gate_ref[0]
    o_ref[0] = ocmp_ref[0] * gt[:, 0:1] + o_sel * gt[:, 1:2] + o_win * gt[:, 2:3]


def _nsa_sample_attn(page_ids, idx, cache_t, q16, ocmp, gate16, win_t, new_rows, layer, past):
    bsz = q16.shape[0]
    nwin = win_t.shape[3]
    return pl.pallas_call(
        functools.partial(_nsa_sample_attn_kernel, past=past, nb=bsz),
        out_shape=jax.ShapeDtypeStruct((bsz, 16, LANES), F32),
        grid_spec=pltpu.PrefetchScalarGridSpec(
            num_scalar_prefetch=2,
            grid=(bsz,),
            in_specs=[pl.BlockSpec(memory_space=pl.ANY),
                      pl.BlockSpec((1, 16, LANES), lambda b, pt, ix: (b, 0, 0)),
                      pl.BlockSpec((1, 16, LANES), lambda b, pt, ix: (b, 0, 0)),
                      pl.BlockSpec((1, 16, LANES), lambda b, pt, ix: (b, 0, 0)),
                      pl.BlockSpec((None, 1, 2 * KV_W, nwin), lambda b, pt, ix: (layer, b, 0, 0)),
                      pl.BlockSpec((1, 8, 512), lambda b, pt, ix: (b, 0, 0))],
            out_specs=pl.BlockSpec((1, 16, LANES), lambda b, pt, ix: (b, 0, 0)),
            scratch_shapes=[pltpu.VMEM((2, N_KV, 2 * KV_W, N_SELECT * PAGE), F32),
                            pltpu.SemaphoreType.DMA((2,))],
        ),
        compiler_params=pltpu.CompilerParams(dimension_semantics=("arbitrary",)),
        name="nsa_sample_attn",
    )(page_ids, idx, cache_t, q16, ocmp, gate16, win_t, new_rows)


def _pool_kernel(u_ref, halo_ref, pw_ref, ps_ref, y_ref, ext_ref, *, tm):
    i = pl.program_id(1)
    hb = POOL_BUF + 1
    ext_ref[0:hb, :] = jnp.where(i > 0, halo_ref[0], 0.0)
    ext_ref[hb:hb + tm, :] = u_ref[0]
    r = i * tm + lax.broadcasted_iota(jnp.int32, (tm, 1), 0)
    for g, w in enumerate(POOL_WINDOWS):
        cols = slice(g * POOL_GW, (g + 1) * POOL_GW)
        acc = ext_ref[hb:hb + tm, cols]
        for k in range(1, w):
            acc = acc + ext_ref[hb - k:hb - k + tm, cols]
        cnt = jnp.minimum(r + 1, w).astype(F32)
        d = acc / cnt - ext_ref[hb:hb + tm, cols]
        y_ref[0, :, cols] = _dot(d.astype(BF16), pw_ref[g]) * ps_ref[:, cols]


def _pool(u, pw, ps, tm):
    bsz, seq = u.shape[:2]
    hb = POOL_BUF + 1
    ratio = tm // hb
    return pl.pallas_call(
        functools.partial(_pool_kernel, tm=tm),
        out_shape=jax.ShapeDtypeStruct(u.shape, F32),
        grid=(bsz, seq // tm),
        in_specs=[pl.BlockSpec((1, tm, D_POOL), lambda b, i: (b, i, 0)),
                  pl.BlockSpec((1, hb, D_POOL), lambda b, i: (b, jnp.maximum(i * ratio - 1, 0), 0)),
                  pl.BlockSpec(pw.shape, lambda b, i: (0, 0, 0)),
                  pl.BlockSpec(ps.shape, lambda b, i: (0, 0))],
        out_specs=pl.BlockSpec((1, tm, D_POOL), lambda b, i: (b, i, 0)),
        scratch_shapes=[pltpu.VMEM((hb + tm, D_POOL), F32)],
        compiler_params=pltpu.CompilerParams(dimension_semantics=("arbitrary", "arbitrary")),
        name="pool",
    )(u, u, pw, ps)


def _mixer_out_kernel(h_ref, py_ref, gp_ref, at_ref, ga_ref, p_ref, wo_ref, wg_ref, wp_ref, o_ref):
    mp = (py_ref[...] * _silu(gp_ref[...])).astype(BF16)
    ma = (at_ref[...] * _silu(ga_ref[...])).astype(BF16)
    h1 = h_ref[...] + _dot(mp, wo_ref[0:D_POOL, :]) + _dot(ma, wo_ref[D_POOL:D_POOL + D_ATTN, :])
    gate = jax.nn.sigmoid(_dot(h1.astype(BF16), wg_ref[...]))
    o_ref[...] = h1 + gate * _dot(p_ref[...].astype(BF16), wp_ref[...])


def _mixer_out(h, py, gp, at, ga, p_all, layer, wo, wg, wp, tm):
    n = h.shape[0]
    row = lambda a: pl.BlockSpec((tm, a.shape[1]), lambda i: (i, 0))
    full = lambda a: pl.BlockSpec(a.shape, lambda i: (0, 0))
    p_spec = pl.BlockSpec((None, tm, p_all.shape[2]), lambda i: (layer, i, 0))
    return pl.pallas_call(
        _mixer_out_kernel,
        out_shape=jax.ShapeDtypeStruct(h.shape, F32),
        grid=(n // tm,),
        in_specs=[row(h), row(py), row(gp), row(at), row(ga), p_spec, full(wo), full(wg), full(wp)],
        out_specs=pl.BlockSpec((tm, D_MODEL), lambda i: (i, 0)),
        compiler_params=pltpu.CompilerParams(dimension_semantics=("arbitrary",),
                                             vmem_limit_bytes=VMEM_LIMIT),
        name="mixer_out",
    )(h, py, gp, at, ga, p_all, wo, wg, wp)


def _rope_tables(pos):
    half = ROPE_DIM // 2
    inv = ROPE_THETA ** (-jnp.arange(0, ROPE_DIM, 2, dtype=F32) / ROPE_DIM)
    ang = pos.astype(F32)[:, None] * inv[None, :]
    cos, sin = jnp.cos(ang), jnp.sin(ang)
    n = pos.shape[0]
    rest = HEAD_DIM - ROPE_DIM
    c = jnp.concatenate([cos, cos, jnp.ones((n, rest), F32)], axis=1)
    s1 = jnp.concatenate([jnp.zeros((n, half), F32), sin, jnp.zeros((n, rest), F32)], axis=1)
    s2 = jnp.concatenate([-sin, jnp.zeros((n, half + rest), F32)], axis=1)
    return tuple(jnp.tile(t, (1, LANES // HEAD_DIM)) for t in (c, s1, s2))


def _cmp_to_sel(nc, n_cmp, nl):
    ci = jnp.arange(nc)[:, None] * CMP_STRIDE
    sj = jnp.arange(nl)[None, :] * SEL_LEN
    m = (ci < sj + SEL_LEN) & (ci + CMP_LEN > sj) & (jnp.arange(nc)[:, None] < n_cmp)
    return m.astype(BF16)


def _layer_params(l, norm_g, w_in, q_norm, k_norm, cmp_pe, cmp_w1, cmp_w2, pool_w, pool_scale, w_out,
                  ple_gate, ple_proj):
    w = w_in[l]
    pad = jnp.zeros((D_MODEL, D_INR - 2840), w.dtype)
    w_r = jnp.concatenate([w[:, 0:2304], w[:, 2328:2840], w[:, 2304:2328], pad], axis=1).astype(BF16)
    ones = jnp.ones((KV_W,), F32)
    gain = jnp.concatenate([jnp.tile(q_norm[l], N_HEADS) * Q_SCALE,
                            jnp.tile(k_norm[l, 0], N_KV), jnp.tile(k_norm[l, 1], N_KV),
                            jnp.tile(k_norm[l, 2], N_KV), ones, ones, ones])[None, :].astype(F32)
    eye = jnp.eye(N_KV, dtype=F32)

    def taps(w1):
        w1r = w1.reshape(2, CMP_STRIDE, HEAD_DIM, HEAD_DIM)
        return jnp.einsum('gh,atde->tgdahe', eye, w1r).reshape(CMP_STRIDE, KV_W, 2 * KV_W).astype(BF16)

    w2 = jnp.stack([jnp.kron(eye, cmp_w2[l, i]) for i in range(2)]).astype(BF16)
    pe2 = jnp.concatenate([jnp.tile(cmp_pe[l, 0], (1, N_KV)), jnp.tile(cmp_pe[l, 1], (1, N_KV))], axis=1)
    return dict(ng=norm_g[l][None, :], w=w_r, gain=gain, wk=taps(cmp_w1[l, 0]), wv=taps(cmp_w1[l, 1]),
                w2=w2, pe2=pe2, pw=pool_w[l].astype(BF16), ps=pool_scale[l][None, :],
                wo=w_out[l].astype(BF16), wg=ple_gate[l].astype(BF16), wp=ple_proj[l].astype(BF16))


def kernel(x_prompt, x_sample, cache_kv, state_win, state_pool, page_table, p_prompt, p_sample,
           norm_g, w_in, q_norm, k_norm, cmp_pe, cmp_w1, cmp_w2, pool_w, pool_scale, w_out,
           ple_gate, ple_proj):
    depth = w_in.shape[0]
    bsz, seq, _ = x_prompt.shape
    dec_b, n_pages = page_table.shape
    n_pool, page_size = cache_kv.shape[1:3]
    past = n_pages * page_size
    n_win = state_win.shape[2]
    assert page_size == PAGE and seq % 512 == 0 and seq >= WINDOW + 128 and past >= WINDOW
    assert x_sample.shape[1] == 1 and past % SEL_LEN == 0 and n_win % LANES == 0
    n_sel_p = seq // SEL_LEN
    n_sel_s = past // SEL_LEN + 1
    assert N_SELECT <= n_sel_p <= LANES
    nl_s = -(-n_sel_s // LANES) * LANES
    nc_p, nc_s = seq // CMP_STRIDE, past // CMP_STRIDE

    tm, tm_out = 256, 512
    tabs_p = _rope_tables(jnp.arange(seq))
    tabs_s = _rope_tables(jnp.full((dec_b,), past, jnp.int32))
    jm = jnp.kron(jnp.eye(2 * LANES // HEAD_DIM, dtype=F32), jnp.full((HEAD_DIM, HEAD_DIM), 1.0 / HEAD_DIM, F32)).astype(BF16)
    c2st_p = _cmp_to_sel(nc_p, nc_p - 1, LANES).T
    c2s_s = _cmp_to_sel(nc_s, nc_s - 1, nl_s)
    ids_p = jnp.arange(bsz * seq // PAGE, dtype=jnp.int32).reshape(bsz, seq // PAGE)
    cache_t = jnp.transpose(cache_kv, (0, 1, 3, 4, 5, 2)).reshape(depth * n_pool, 4 * KV_W, PAGE)
    win_t = jnp.transpose(state_win, (0, 1, 3, 4, 5, 2)).reshape(depth, dec_b, 2 * KV_W, n_win)

    hp = x_prompt.reshape(bsz * seq, D_MODEL)
    hs = x_sample.reshape(dec_b, D_MODEL)
    outs = [[] for _ in range(6)]
    for l in range(depth):
        prm = _layer_params(l, norm_g, w_in, q_norm, k_norm, cmp_pe, cmp_w1, cmp_w2, pool_w, pool_scale,
                            w_out, ple_gate, ple_proj)
        u, gp, qpad, kv, win, ga, gbr, kaug0, kaug1, vsb, kwb, vwb = _mixer_in(
            hp, tabs_p, prm['ng'], prm['w'], prm['gain'], jm, tm)
        ck, _, cvt = _compress(ids_p, kv.reshape(bsz * seq // PAGE, PAGE, 512), prm['wk'], prm['wv'], prm['pe2'],
                               prm['w2'], False)
        r3 = lambda a: a.reshape(bsz, seq, a.shape[-1])
        attn = _nsa_prompt(r3(qpad), r3(gbr), ck, cvt, r3(kaug0), r3(kaug1), r3(vsb), r3(kwb), r3(vwb), c2st_p, 256, 512)
        py = _pool(r3(u), prm['pw'], prm['ps'], tm_out)
        hp = _mixer_out(hp, py.reshape(bsz * seq, D_POOL), gp, attn.reshape(bsz * seq, D_ATTN), ga,
                        p_prompt.reshape(depth, bsz * seq, D_PLE), l, prm['wo'], prm['wg'], prm['wp'], tm_out)
        outs[0].append(kv.reshape(bsz, seq, 4, N_KV, HEAD_DIM))
        outs[2].append(r3(win)[:, seq - min(WINDOW, seq):].reshape(bsz, min(WINDOW, seq), 2, N_KV, HEAD_DIM))
        outs[4].append(r3(u)[:, seq - POOL_BUF:])

        u, gp, qpad, kv, win, ga, gbr, _, _, _, _, _ = _mixer_in(
            hs, tabs_s, prm['ng'], prm['w'], prm['gain'], jm, dec_b)
        ids_s = page_table + l * n_pool
        ck, cv, _ = _compress(ids_s, cache_t, prm['wk'], prm['wv'], prm['pe2'], prm['w2'], True)
        q16 = jnp.pad(qpad.reshape(dec_b, N_HEADS, LANES), ((0, 0), (0, 16 - N_HEADS), (0, 0)))
        ocmp, idx = _nsa_sample_cmp(q16, ck, cv, c2s_s, past)
        gate16 = jnp.pad(gbr[:, :3 * N_HEADS].reshape(dec_b, N_HEADS, 3), ((0, 0), (0, 16 - N_HEADS), (0, LANES - 3)))
        new_rows = jnp.broadcast_to(jnp.concatenate([kv[:, 256:512], win], axis=1)[:, None, :], (dec_b, 8, 512))
        idx = idx.reshape(dec_b, 8, LANES)[:, :N_KV, :N_SELECT].reshape(dec_b, N_KV * N_SELECT)
        o16 = _nsa_sample_attn(ids_s, idx, cache_t, q16, ocmp,
                               gate16, win_t, new_rows, l, past)
        attn = jnp.concatenate([o16[:, h, (h // Q_PER_KV) * HEAD_DIM:(h // Q_PER_KV + 1) * HEAD_DIM]
                                for h in range(N_HEADS)], axis=1)
        u_ext = jnp.concatenate([state_pool[l], u[:, None, :]], axis=1)
        py = _pool(u_ext, prm['pw'], prm['ps'], POOL_BUF + 1)[:, POOL_BUF]
        hs = _mixer_out(hs, py, gp, attn, ga, p_sample.reshape(depth, dec_b, D_PLE), l, prm['wo'], prm['wg'], prm['wp'], dec_b)
        outs[1].append(kv.reshape(dec_b, 1, 4, N_KV, HEAD_DIM))
        outs[3].append(win.reshape(dec_b, 1, 2, N_KV, HEAD_DIM))
        outs[5].append(u_ext[:, 1:])

    return (hp.reshape(bsz, seq, D_MODEL), hs.reshape(dec_b, 1, D_MODEL),
            jnp.stack(outs[0]), jnp.stack(outs[1]), jnp.stack(outs[2]),
            jnp.concatenate([state_win[:, :, 1:], jnp.stack(outs[3])], axis=2),
            jnp.stack(outs[4]), jnp.stack(outs[5]))
```
